```python
import functools
import jax, jax.numpy as jnp
from jax import lax
import numpy as np

D_MODEL = 1024
BATCH = 4
SEQ = 8192
DEPTH = 4

MEM_LEN = 256
Q_BLOCK = 128
MAX_GROUPS = 8
N_BRANCHES = 4
BRANCH_WIDTH = 256
MLA_HEADS = 4
MLA_NOPE = 64
MLA_ROPE = 32
MLA_V = 64
MLA_Q_RANK = 384
MLA_KV_RANK = 256
ROPE_THETA = 10000.0
SB_HEADS = 4
SB_HEAD_DIM = 64
FOX_HEADS = 4
FOX_HEAD_DIM = 64
MEM_HEADS = 4
MEM_HEAD_DIM = 64
MERGE_RANK = 128
RMS_EPS = 1e-6
LN_EPS = 1e-5
DEEPNORM_ALPHA = (2 * DEPTH) ** 0.25
DEEPNORM_BETA = (8 * DEPTH) ** -0.25

IN_SPLITS = (
    MLA_Q_RANK,
    MLA_KV_RANK,
    MLA_ROPE,
    3 * SB_HEADS * SB_HEAD_DIM,
    3 * FOX_HEADS * FOX_HEAD_DIM,
    FOX_HEADS,
    MEM_HEADS * MEM_HEAD_DIM,
    N_BRANCHES * BRANCH_WIDTH,
    MERGE_RANK,
)
IN_WIDTH = sum(IN_SPLITS)

kernel_name = "hybrid_mla_stickbreak_fox_memory_deepnorm"


def _split_last(t, sizes):
    out, start = [], 0
    for n in sizes:
        out.append(t[..., start:start + n])
        start += n
    return out


def _rms_norm(x, g):
    xf = x.astype(jnp.float32)
    y = xf * lax.rsqrt(jnp.mean(xf * xf, axis=-1, keepdims=True) + RMS_EPS)
    return (y * g.astype(jnp.float32)).astype(x.dtype)


def _layer_norm(x, g, b):
    xf = x.astype(jnp.float32)
    mu = jnp.mean(xf, axis=-1, keepdims=True)
    var = jnp.mean(jnp.square(xf - mu), axis=-1, keepdims=True)
    y = (xf - mu) * lax.rsqrt(var + LN_EPS)
    return (y * g.astype(jnp.float32) + b.astype(jnp.float32)).astype(x.dtype)


def _rope(x, cos, sin):
    x1, x2 = jnp.split(x, 2, axis=-1)
    return jnp.concatenate([x1 * cos - x2 * sin, x1 * sin + x2 * cos], axis=-1)


def _causal_sweep(block_fn, s):
    n_blocks = s // Q_BLOCK
    n_groups = min(MAX_GROUPS, n_blocks)
    bounds = [(g * n_blocks) // n_groups for g in range(n_groups + 1)]
    outs = []
    for g in range(n_groups):
        b0, b1 = bounds[g], bounds[g + 1]
        o = lax.map(functools.partial(block_fn, kend=b1 * Q_BLOCK), jnp.arange(b0, b1))
        nb, b, qn, h, dv = o.shape
        outs.append(o.transpose(1, 0, 2, 3, 4).reshape(b, nb * qn, h, dv))
    return jnp.concatenate(outs, axis=1)


def _causal_softmax_attention(q, k, v, log_decay=None):
    b, s, h, d = q.shape
    q = q * (d ** -0.5)
    c_t = None if log_decay is None else log_decay.transpose(0, 2, 1)

    def block(i, kend):
        start = i * Q_BLOCK
        qb = lax.dynamic_slice_in_dim(q, start, Q_BLOCK, axis=1)
        kb, vb = k[:, :kend], v[:, :kend]
        logits = jnp.einsum('bqhd,bkhd->bhqk', qb, kb, preferred_element_type=jnp.float32)
        if c_t is not None:
            cq = lax.dynamic_slice_in_dim(c_t, start, Q_BLOCK, axis=2)
            logits = logits + (cq[..., :, None] - c_t[..., None, :kend])
        causal = jnp.arange(kend)[None, :] <= (start + jnp.arange(Q_BLOCK))[:, None]
        logits = jnp.where(causal, logits, -jnp.inf)
        m = jnp.max(logits, axis=-1, keepdims=True)
        p = jnp.exp(logits - m)
        denom = jnp.sum(p, axis=-1).transpose(0, 2, 1)[..., None]
        o = jnp.einsum('bhqk,bkhd->bqhd', p.astype(v.dtype), vb, preferred_element_type=jnp.float32)
        return (o / denom).astype(v.dtype)

    return _causal_sweep(block, s)


def _stick_breaking_attention(q, k, v):
    b, s, h, d = q.shape
    q = q * (d ** -0.5)
    tri = jnp.tril(jnp.ones((Q_BLOCK, Q_BLOCK), jnp.float32))

    def block(i, kend):
        start = i * Q_BLOCK
        nk = kend // Q_BLOCK
        qb = lax.dynamic_slice_in_dim(q, start, Q_BLOCK, axis=1)
        kb, vb = k[:, :kend], v[:, :kend]
        z = jnp.einsum('bqhd,bkhd->bhqk', qb, kb, preferred_element_type=jnp.float32)
        strict = jnp.arange(kend)[None, :] < (start + jnp.arange(Q_BLOCK))[:, None]
        log_keep = jnp.where(strict, jax.nn.log_sigmoid(-z), 0.0)
        local = jnp.einsum('bhqnc,cr->bhqnr', log_keep.reshape(b, h, Q_BLOCK, nk, Q_BLOCK), tri)
        tot = local[..., 0]
        cross = lax.cumsum(tot, axis=3, reverse=True) - tot
        incl = (local + cross[..., None]).reshape(b, h, Q_BLOCK, kend)
        a = jnp.where(strict, jnp.exp(z + incl), 0.0).astype(v.dtype)
        return jnp.einsum('bhqk,bkhd->bqhd', a, vb)

    return _causal_sweep(block, s)


def _memory_attention(q, k, v):
    logits = jnp.einsum('bshd,bmhd->bhsm', q * (q.shape[-1] ** -0.5), k,
                        preferred_element_type=jnp.float32)
    p = jax.nn.softmax(logits, axis=-1).astype(v.dtype)
    return jnp.einsum('bhsm,bmhd->bshd', p, v)


def _layer(x, mem, cos, sin, w_in, q_norm, w_qb, kv_norm, w_kvb, fox_bias,
           w_mem_kv, w_merge_up, w_branch, w_out, ln_g, ln_b):
    b, s, _ = x.shape
    h = x @ w_in
    (c_q, c_kv, k_rope, sb_qkv, fox_qkv, fox_f, mem_q, gate_z, merge_r) = _split_last(h, IN_SPLITS)

    q = (_rms_norm(c_q, q_norm) @ w_qb).reshape(b, s, MLA_HEADS, MLA_NOPE + MLA_ROPE)
    q_nope, q_pe = q[..., :MLA_NOPE], q[..., MLA_NOPE:]
    q_pe = _rope(q_pe, cos[:, :, None, :], sin[:, :, None, :])
    kv = (_rms_norm(c_kv, kv_norm) @ w_kvb).reshape(b, s, MLA_HEADS, MLA_NOPE + MLA_V)
    k_nope, v_mla = kv[..., :MLA_NOPE], kv[..., MLA_NOPE:]
    k_pe = _rope(k_rope[:, :, None, :], cos[:, :, None, :], sin[:, :, None, :])
    k_pe = jnp.broadcast_to(k_pe, (b, s, MLA_HEADS, MLA_ROPE))
    y_mla = _causal_softmax_attention(jnp.concatenate([q_nope, q_pe], -1),
                                      jnp.concatenate([k_nope, k_pe], -1), v_mla)

    sq, sk, sv = [t.reshape(b, s, SB_HEADS, SB_HEAD_DIM) for t in jnp.split(sb_qkv, 3, axis=-1)]
    y_sb = _stick_breaking_attention(sq, sk, sv)

    fq, fk, fv = [t.reshape(b, s, FOX_HEADS, FOX_HEAD_DIM) for t in jnp.split(fox_qkv, 3, axis=-1)]
    log_f = jax.nn.log_sigmoid(fox_f.astype(jnp.float32) + fox_bias.astype(jnp.float32))
    c = jnp.cumsum(log_f, axis=1)
    y_fox = _causal_softmax_attention(fq, fk, fv, log_decay=c)

    mk, mv = jnp.split(mem @ w_mem_kv, 2, axis=-1)
    mk = mk.reshape(b, MEM_LEN, MEM_HEADS, MEM_HEAD_DIM)
    mv = mv.reshape(b, MEM_LEN, MEM_HEADS, MEM_HEAD_DIM)
    y_mem = _memory_attention(mem_q.reshape(b, s, MEM_HEADS, MEM_HEAD_DIM), mk, mv)

    branches = (y_mla, y_sb, y_fox, y_mem)
    gate_z = gate_z.reshape(b, s, N_BRANCHES, BRANCH_WIDTH)
    merge_g = jax.nn.sigmoid(merge_r @ w_merge_up).reshape(b, s, N_BRANCHES, D_MODEL)
    merged = jnp.zeros_like(x)
    for n in range(N_BRANCHES):
        yb = branches[n].reshape(b, s, BRANCH_WIDTH) * jax.nn.silu(gate_z[:, :, n])
        merged = merged + merge_g[:, :, n] * (yb @ w_branch[n])
    out = merged @ w_out

    return _layer_norm(DEEPNORM_ALPHA * x + out, ln_g, ln_b)


def setup_inputs(seed: int = 0) -> dict:
    key = jax.random.key(seed)
    ks = jax.random.split(key, 16)
    f32 = jnp.float32

    def nrm(k, shape, scale):
        return jax.random.normal(k, shape, f32) * scale

    x = nrm(ks[0], (BATCH, SEQ, D_MODEL), 1.0)
    mem = nrm(ks[1], (BATCH, MEM_LEN, D_MODEL), 1.0)
    offsets = jax.random.randint(ks[2], (BATCH, 1), 0, 4096, dtype=jnp.int32)
    positions = (jnp.arange(SEQ, dtype=jnp.int32)[None, :] + offsets).astype(jnp.int32)

    w_in = nrm(ks[3], (DEPTH, D_MODEL, IN_WIDTH), D_MODEL ** -0.5)
    mla_q_norm = 1.0 + nrm(ks[4], (DEPTH, MLA_Q_RANK), 0.02)
    mla_w_qb = nrm(ks[5], (DEPTH, MLA_Q_RANK, MLA_HEADS * (MLA_NOPE + MLA_ROPE)), MLA_Q_RANK ** -0.5)
    mla_kv_norm = 1.0 + nrm(ks[6], (DEPTH, MLA_KV_RANK), 0.02)
    mla_w_kvb = nrm(ks[7], (DEPTH, MLA_KV_RANK, MLA_HEADS * (MLA_NOPE + MLA_V)), MLA_KV_RANK ** -0.5)
    fox_forget_bias = jax.random.uniform(ks[8], (DEPTH, FOX_HEADS), f32, 1.0, 4.0)
    w_mem_kv = nrm(ks[9], (DEPTH, D_MODEL, 2 * MEM_HEADS * MEM_HEAD_DIM), D_MODEL ** -0.5)
    w_merge_up = nrm(ks[10], (DEPTH, MERGE_RANK, N_BRANCHES * D_MODEL), MERGE_RANK ** -0.5)
    w_branch = nrm(ks[11], (DEPTH, N_BRANCHES, BRANCH_WIDTH, D_MODEL), BRANCH_WIDTH ** -0.5 * DEEPNORM_BETA)
    w_out = nrm(ks[12], (DEPTH, D_MODEL, D_MODEL), D_MODEL ** -0.5 * DEEPNORM_BETA)
    ln_gain = 1.0 + nrm(ks[13], (DEPTH, D_MODEL), 0.02)
    ln_bias = nrm(ks[14], (DEPTH, D_MODEL), 0.02)
    return {"x": x, "mem": mem, "positions": positions, "w_in": w_in,
            "mla_q_norm": mla_q_norm, "mla_w_qb": mla_w_qb,
            "mla_kv_norm": mla_kv_norm, "mla_w_kvb": mla_w_kvb,
            "fox_forget_bias": fox_forget_bias, "w_mem_kv": w_mem_kv,
            "w_merge_up": w_merge_up, "w_branch": w_branch, "w_out": w_out,
            "ln_gain": ln_gain, "ln_bias": ln_bias}


def reference(x, mem, positions, w_in, mla_q_norm, mla_w_qb, mla_kv_norm, mla_w_kvb,
              fox_forget_bias, w_mem_kv, w_merge_up, w_branch, w_out, ln_gain, ln_bias):
    inv_freq = ROPE_THETA ** (-jnp.arange(0, MLA_ROPE, 2, dtype=jnp.float32) / MLA_ROPE)
    ang = positions.astype(jnp.float32)[..., None] * inv_freq
    cos = jnp.cos(ang).astype(x.dtype)
    sin = jnp.sin(ang).astype(x.dtype)
    for l in range(DEPTH):
        x = _layer(x, mem, cos, sin, w_in[l], mla_q_norm[l], mla_w_qb[l],
                   mla_kv_norm[l], mla_w_kvb[l], fox_forget_bias[l], w_mem_kv[l],
                   w_merge_up[l], w_branch[l], w_out[l], ln_gain[l], ln_bias[l])
    return x
```

```python
import functools

import jax
import jax.numpy as jnp
import numpy as np
from jax import lax
from jax.experimental import pallas as pl
from jax.experimental.pallas import tpu as pltpu

F32, BF16 = jnp.float32, jnp.bfloat16

D_MODEL = 1024
DEPTH = 4
MEM_LEN = 256
N_BRANCHES = 4
BRANCH_WIDTH = 256
HEADS = 4
HEAD_V = 64
MLA_NOPE = 64
MLA_ROPE = 32
MLA_Q_RANK = 384
MLA_KV_RANK = 256
ROPE_THETA = 10000.0
SB_HEAD_DIM = 64
MERGE_RANK = 128
RMS_EPS = 1e-6
LN_EPS = 1e-5
DEEPNORM_ALPHA = (2 * DEPTH) ** 0.25

_C_CQ, _C_CKV, _C_KR = 0, 384, 640
_C_SQ, _C_SK, _C_SV = 672, 928, 1184
_C_FQ, _C_FK, _C_FV = 1440, 1696, 1952
_C_FF, _C_MQ, _C_GATE, _C_MR = 2208, 2212, 2468, 3492

HEAD_PAD = 128
TM = 512
TQ = 512
TK = 256
NEG = -1e30
VMEM_LIMIT = 56 * 1024 * 1024

_T_CQ, _T_CKV, _T_SQ, _T_SV, _T_FQ, _T_FV, _T_MQ, _T_FF, _T_END = 0, 384, 640, 896, 1152, 1408, 1664, 1920, 1936
_R_CKV, _R_ROPE, _R_SK, _R_FK, _R_GATE, _R_MR, _R_END = 0, 256, 384, 896, 1408, 2432, 2560
AUG = 16


def _dot(a, b):
    return jnp.dot(a, b, preferred_element_type=F32)


def _dot_nt(a, b):
    return lax.dot_general(a, b, (((1,), (1,)), ((), ())), preferred_element_type=F32)


def _split3(v):
    p1 = v.astype(BF16).astype(F32)
    r1 = v - p1
    p2 = r1.astype(BF16).astype(F32)
    p3 = (r1 - p2).astype(BF16).astype(F32)
    return p1, p2, p3


def _rope_body(pos_ref, invf_ref, cos_ref, sin_ref):
    ang = pos_ref[0].astype(F32) * invf_ref[...]
    cos_ref[0] = jnp.cos(ang)
    sin_ref[0] = jnp.sin(ang)


def _rope_tables(positions):
    b, s = positions.shape
    half = MLA_ROPE // 2
    inv_freq = ROPE_THETA ** (-jnp.arange(0, MLA_ROPE, 2, dtype=F32) / MLA_ROPE)
    return pl.pallas_call(
        _rope_body,
        grid=(b,),
        in_specs=[pl.BlockSpec((1, 1, s), lambda i: (i, 0, 0)),
                  pl.BlockSpec((half, 1), lambda i: (0, 0))],
        out_specs=[pl.BlockSpec((1, half, s), lambda i: (i, 0, 0))] * 2,
        out_shape=[jax.ShapeDtypeStruct((b, half, s), F32)] * 2,
        name="rope_tables",
    )(positions.reshape(b, 1, s), inv_freq.reshape(half, 1))


def _memkv_body(mem_ref, w_ref, o_ref):
    o_ref[0] = _dot(mem_ref[...].astype(BF16), w_ref[0].astype(BF16))


def _memkv(mem, w_mem_kv):
    b, m, d = mem.shape
    depth, _, n = w_mem_kv.shape
    return pl.pallas_call(
        _memkv_body,
        grid=(depth,),
        in_specs=[pl.BlockSpec((b * m, d), lambda l: (0, 0)),
                  pl.BlockSpec((1, d, n), lambda l: (l, 0, 0))],
        out_specs=pl.BlockSpec((1, b * m, n), lambda l: (l, 0, 0)),
        out_shape=jax.ShapeDtypeStruct((depth, b * m, n), F32),
        compiler_params=pltpu.CompilerParams(vmem_limit_bytes=VMEM_LIMIT),
        name="mem_kv_proj",
    )(mem.reshape(b * m, d), w_mem_kv)


def _proj_body(x_ref, wt_ref, wr_ref, qn_ref, kvnc_ref, kvnr_ref, wqbt_ref, wvt_ref, wkmla_ref, p2_ref,
               cos_ref, sin_ref, tab_ref, fbias_ref, selq_ref, onesq_ref, selk_ref, onesk_ref,
               kmem_ref, vmemt_ref,
               qmla_o, kmla_o, vmla_o, qsb_o, ksb_o, vsb_o, qfox_o, kfox_o, vfox_o, ymem_o, gate_o, mr_o,
               carry_ref):
    tm = x_ref.shape[1]
    xb = x_ref[0].astype(BF16)

    def store_vt(o_ref, vt):
        for j in range(tm // TK):
            o_ref[0, j] = vt[:, j * TK:(j + 1) * TK].astype(BF16)

    cq = _dot_nt(wt_ref[_T_CQ:_T_CKV, :], xb)
    cqn = cq * lax.rsqrt(jnp.mean(cq * cq, axis=0, keepdims=True) + RMS_EPS) * qn_ref[...]
    q = _dot(wqbt_ref[...], cqn.astype(BF16)) * ((MLA_NOPE + MLA_ROPE) ** -0.5)
    cos_t, sin_t = cos_ref[0], sin_ref[0]
    half = MLA_ROPE // 2
    dq = MLA_NOPE + MLA_ROPE
    zq = jnp.zeros((HEAD_PAD - dq, tm), F32)
    for h in range(HEADS):
        nope = q[h * dq:h * dq + MLA_NOPE]
        x1 = q[h * dq + MLA_NOPE:h * dq + MLA_NOPE + half]
        x2 = q[h * dq + MLA_NOPE + half:(h + 1) * dq]
        blk = jnp.concatenate([nope, x1 * cos_t - x2 * sin_t, x1 * sin_t + x2 * cos_t, zq], axis=0)
        qmla_o[0, h * HEAD_PAD:(h + 1) * HEAD_PAD, :] = blk.astype(BF16)

    ckv_t = _dot_nt(wt_ref[_T_CKV:_T_SQ, :], xb)
    ckvn_t = ckv_t * lax.rsqrt(jnp.mean(ckv_t * ckv_t, axis=0, keepdims=True) + RMS_EPS) * kvnc_ref[...]
    store_vt(vmla_o, _dot(wvt_ref[...], ckvn_t.astype(BF16)))
    ckv_r = _dot(xb, wr_ref[:, _R_CKV:_R_ROPE])
    ckvn_r = ckv_r * lax.rsqrt(jnp.mean(ckv_r * ckv_r, axis=1, keepdims=True) + RMS_EPS) * kvnr_ref[...]
    prod = _dot(xb, wr_ref[:, _R_ROPE:_R_SK]) * tab_ref[0]
    kpe = prod + pltpu.roll(prod, 128 - MLA_ROPE, 1)
    kmla = _dot(ckvn_r.astype(BF16), wkmla_ref[...]) + _dot(kpe.astype(BF16), p2_ref[...])
    kmla_o[0] = kmla.astype(BF16)

    sq = _dot_nt(wt_ref[_T_SQ:_T_SV, :], xb) * (SB_HEAD_DIM ** -0.5)
    zs = jnp.zeros((HEAD_PAD - SB_HEAD_DIM, tm), F32)
    for h in range(HEADS):
        blk = jnp.concatenate([sq[h * SB_HEAD_DIM:(h + 1) * SB_HEAD_DIM], zs], axis=0)
        qsb_o[0, h * HEAD_PAD:(h + 1) * HEAD_PAD, :] = blk.astype(BF16)
    store_vt(vsb_o, _dot_nt(wt_ref[_T_SV:_T_FQ, :], xb))
    ksb_o[0] = _dot(xb, wr_ref[:, _R_SK:_R_FK]).astype(BF16)

    ff = _dot_nt(wt_ref[_T_FF:_T_END, :], xb) + fbias_ref[...]
    lf = jnp.minimum(ff, 0.0) - jnp.log1p(jnp.exp(-jnp.abs(ff)))
    l1, l2, l3 = _split3(lf)
    upper = (lax.broadcasted_iota(jnp.int32, (tm, tm), 0) <= lax.broadcasted_iota(jnp.int32, (tm, tm), 1))
    csum = _dot(jnp.concatenate([l1, l2, l3], axis=0).astype(BF16), upper.astype(BF16))
    @pl.when(pl.program_id(1) == 0)
    def _():
        carry_ref[...] = jnp.zeros_like(carry_ref)

    c = csum[0:AUG] + csum[AUG:2 * AUG] + csum[2 * AUG:3 * AUG] + carry_ref[:, 0:1]
    carry_ref[...] = jnp.broadcast_to(c[:, tm - 1:tm], carry_ref.shape)
    c1, c2, c3 = _split3(c)
    cst = jnp.concatenate([c1, c2, c3, jnp.zeros((HEAD_PAD - 3 * AUG, tm), F32)], axis=0)
    aug_t = _dot(selq_ref[...], cst.astype(BF16)) + onesq_ref[...]
    fq = _dot_nt(wt_ref[_T_FQ:_T_FV, :], xb) * (SB_HEAD_DIM ** -0.5)
    zf = jnp.zeros((HEAD_PAD - SB_HEAD_DIM - 8, tm), F32)
    for h in range(HEADS):
        blk = jnp.concatenate([fq[h * SB_HEAD_DIM:(h + 1) * SB_HEAD_DIM], aug_t[h * 8:(h + 1) * 8], zf], axis=0)
        qfox_o[0, h * HEAD_PAD:(h + 1) * HEAD_PAD, :] = blk.astype(BF16)
    store_vt(vfox_o, _dot_nt(wt_ref[_T_FV:_T_MQ, :], xb))
    kfox = _dot(xb, wr_ref[:, _R_FK:_R_GATE]) + _dot(cst.T.astype(BF16), selk_ref[...]) + onesk_ref[...]
    kfox_o[0] = kfox.astype(BF16)

    mq = _dot_nt(wt_ref[_T_MQ:_T_FF, :], xb) * (HEAD_V ** -0.5)
    for h in range(HEADS):
        s = _dot(kmem_ref[0, h], mq[h * HEAD_V:(h + 1) * HEAD_V].astype(BF16))
        p = jnp.exp(s - jnp.max(s, axis=0, keepdims=True))
        o = _dot(vmemt_ref[0, h], p.astype(BF16))
        ymem_o[0, h * HEAD_V:(h + 1) * HEAD_V, :] = o / jnp.sum(p, axis=0, keepdims=True)

    gate_o[0] = _dot(xb, wr_ref[:, _R_GATE:_R_MR]).astype(BF16)
    mr_o[0] = _dot(xb, wr_ref[:, _R_MR:_R_END]).astype(BF16)


def _placed(w, n_in):
    d = w.shape[0]
    w = w.reshape(d, HEADS, n_in)
    return jnp.pad(w, ((0, 0), (0, 0), (0, HEAD_PAD - n_in))).reshape(d, HEADS * HEAD_PAD)


def _selection_constants():
    selq = np.zeros((HEADS * 8, HEAD_PAD), np.float32)
    onesq = np.zeros((HEADS * 8, 1), np.float32)
    selk = np.zeros((HEAD_PAD, HEADS * HEAD_PAD), np.float32)
    onesk = np.zeros((1, HEADS * HEAD_PAD), np.float32)
    p2 = np.zeros((HEAD_PAD, HEADS * HEAD_PAD), np.float32)
    for h in range(HEADS):
        for i in range(3):
            selq[h * 8 + i, i * AUG + h] = 1.0
            onesq[h * 8 + 3 + i, 0] = 1.0
            onesk[0, h * HEAD_PAD + SB_HEAD_DIM + i] = 1.0
            selk[i * AUG + h, h * HEAD_PAD + SB_HEAD_DIM + 3 + i] = -1.0
        for j in range(MLA_ROPE):
            p2[j, h * HEAD_PAD + MLA_NOPE + j] = 1.0
    return (jnp.asarray(selq, BF16), jnp.asarray(onesq), jnp.asarray(selk, BF16), jnp.asarray(onesk),
            jnp.asarray(p2, BF16))


def _proj_weights(w_in, q_norm, w_qb, kv_norm, w_kvb, fox_bias):
    cols = lambda a, n: w_in[:, a:a + n]
    wt = jnp.concatenate([
        cols(_C_CQ, 384).T, cols(_C_CKV, 256).T, cols(_C_SQ, 256).T, cols(_C_SV, 256).T,
        cols(_C_FQ, 256).T, cols(_C_FV, 256).T, cols(_C_MQ, 256).T, cols(_C_FF, 4).T,
        jnp.zeros((_T_END - _T_FF - HEADS, D_MODEL), F32)], axis=0).astype(BF16)
    kr = cols(_C_KR, MLA_ROPE)
    half = MLA_ROPE // 2
    rope = jnp.concatenate([kr, -kr[:, half:], kr[:, :half], jnp.zeros((D_MODEL, HEAD_PAD - 2 * MLA_ROPE), F32)], axis=1)
    wr = jnp.concatenate([
        cols(_C_CKV, 256), rope, _placed(cols(_C_SK, 256), SB_HEAD_DIM), _placed(cols(_C_FK, 256), SB_HEAD_DIM),
        cols(_C_GATE, 1024), cols(_C_MR, MERGE_RANK)], axis=1).astype(BF16)
    kvb = w_kvb.reshape(MLA_KV_RANK, HEADS, MLA_NOPE + HEAD_V)
    wkmla = jnp.pad(kvb[:, :, :MLA_NOPE], ((0, 0), (0, 0), (0, HEAD_PAD - MLA_NOPE))).reshape(MLA_KV_RANK, HEADS * HEAD_PAD)
    wvt = kvb[:, :, MLA_NOPE:].reshape(MLA_KV_RANK, HEADS * HEAD_V).T
    fbias = jnp.pad(fox_bias, (0, AUG - HEADS)).reshape(AUG, 1)
    return (wt, wr, q_norm.reshape(-1, 1), kv_norm.reshape(-1, 1), kv_norm.reshape(1, -1),
            w_qb.T.astype(BF16), wvt.astype(BF16), wkmla.astype(BF16), fbias)


def _proj(x, weights, consts, cos_t, sin_t, tab, kmem, vmemt):
    b, s, d = x.shape
    wt, wr, qn, kvnc, kvnr, wqbt, wvt, wkmla, fbias = weights
    selq, onesq, selk, onesk, p2 = consts
    nkt = s // TK
    full = lambda a: pl.BlockSpec(a.shape, lambda i, j: (0,) * a.ndim)
    qt_spec = pl.BlockSpec((1, HEADS * HEAD_PAD, TM), lambda i, j: (i, 0, j))
    k_spec = pl.BlockSpec((1, TM, HEADS * HEAD_PAD), lambda i, j: (i, j, 0))
    vt_spec = pl.BlockSpec((1, TM // TK, HEADS * HEAD_V, TK), lambda i, j: (i, j, 0, 0))
    qt_shape = jax.ShapeDtypeStruct((b, HEADS * HEAD_PAD, s), BF16)
    k_shape = jax.ShapeDtypeStruct((b, s, HEADS * HEAD_PAD), BF16)
    vt_shape = jax.ShapeDtypeStruct((b, nkt, HEADS * HEAD_V, TK), BF16)
    return pl.pallas_call(
        _proj_body,
        grid=(b, s // TM),
        in_specs=[pl.BlockSpec((1, TM, d), lambda i, j: (i, j, 0)),
                  full(wt), full(wr), full(qn), full(kvnc), full(kvnr), full(wqbt), full(wvt), full(wkmla), full(p2),
                  pl.BlockSpec((1, MLA_ROPE // 2, TM), lambda i, j: (i, 0, j)),
                  pl.BlockSpec((1, MLA_ROPE // 2, TM), lambda i, j: (i, 0, j)),
                  pl.BlockSpec((1, TM, HEAD_PAD), lambda i, j: (i, j, 0)),
                  full(fbias), full(selq), full(onesq), full(selk), full(onesk),
                  pl.BlockSpec((1, HEADS, MEM_LEN, HEAD_V), lambda i, j: (i, 0, 0, 0)),
                  pl.BlockSpec((1, HEADS, HEAD_V, MEM_LEN), lambda i, j: (i, 0, 0, 0))],
        out_specs=[qt_spec, k_spec, vt_spec, qt_spec, k_spec, vt_spec, qt_spec, k_spec, vt_spec,
                   pl.BlockSpec((1, HEADS * HEAD_V, TM), lambda i, j: (i, 0, j)),
                   pl.BlockSpec((1, TM, N_BRANCHES * BRANCH_WIDTH), lambda i, j: (i, j, 0)),
                   pl.BlockSpec((1, TM, MERGE_RANK), lambda i, j: (i, j, 0))],
        out_shape=[qt_shape, k_shape, vt_shape, qt_shape, k_shape, vt_shape, qt_shape, k_shape, vt_shape,
                   jax.ShapeDtypeStruct((b, HEADS * HEAD_V, s), F32),
                   jax.ShapeDtypeStruct((b, s, N_BRANCHES * BRANCH_WIDTH), BF16),
                   jax.ShapeDtypeStruct((b, s, MERGE_RANK), BF16)],
        scratch_shapes=[pltpu.VMEM((AUG, 128), F32)],
        compiler_params=pltpu.CompilerParams(dimension_semantics=("arbitrary", "arbitrary"),
                                             vmem_limit_bytes=VMEM_LIMIT),
        name="fused_projection",
    )(x, wt, wr, qn, kvnc, kvnr, wqbt, wvt, wkmla, p2, cos_t, sin_t, tab, fbias, selq, onesq, selk, onesk,
      kmem, vmemt)


def _key_tile(k_ref, k):
    return k_ref[0, pl.ds(pl.multiple_of(k * TK, TK), TK), :]


def _softmax_attn_body(q_ref, k_ref, v_ref, o_ref):
    q_t = q_ref[0]
    n_full = pl.program_id(2) * (TQ // TK)

    def block(k, carry, mask):
        m, l, acc = carry
        s = _dot(_key_tile(k_ref, k), q_t)
        if mask is not None:
            s = jnp.where(mask, s, NEG)
        m_new = jnp.maximum(m, jnp.max(s, axis=0, keepdims=True))
        alpha = jnp.exp(m - m_new)
        p = jnp.exp(s - m_new)
        if mask is not None:
            p = jnp.where(mask, p, 0.0)
        l = alpha * l + jnp.sum(p, axis=0, keepdims=True)
        acc = alpha * acc + _dot(v_ref[0, k], p.astype(BF16))
        return m_new, l, acc

    carry = (jnp.full((1, TQ), NEG, F32), jnp.zeros((1, TQ), F32), jnp.zeros((HEAD_V, TQ), F32))
    carry = lax.fori_loop(0, n_full, lambda k, c: block(k, c, None), carry)
    row = lax.broadcasted_iota(jnp.int32, (TK, TQ), 0)
    col = lax.broadcasted_iota(jnp.int32, (TK, TQ), 1)
    for j in range(TQ // TK):
        carry = block(n_full + j, carry, row + j * TK <= col)
    _, l, acc = carry
    o_ref[0] = acc / l


def _sb_attn_body(q_ref, k_ref, v_ref, tri_ref, o_ref):
    q_t = q_ref[0]
    tri = tri_ref[...]
    n_full = pl.program_id(2) * (TQ // TK)

    def block(k, carry, mask):
        cross, acc = carry
        z = _dot(_key_tile(k_ref, k), q_t)
        log_keep = -(jnp.maximum(z, 0.0) + jnp.log1p(jnp.exp(-jnp.abs(z))))
        if mask is not None:
            log_keep = jnp.where(mask, log_keep, 0.0)
        local = _dot(tri, log_keep.astype(BF16))
        a = jnp.exp(z + local + cross)
        if mask is not None:
            a = jnp.where(mask, a, 0.0)
        acc = acc + _dot(v_ref[0, k], a.astype(BF16))
        return cross + local[0:1, :], acc

    carry = (jnp.zeros((1, TQ), F32), jnp.zeros((HEAD_V, TQ), F32))
    row = lax.broadcasted_iota(jnp.int32, (TK, TQ), 0)
    col = lax.broadcasted_iota(jnp.int32, (TK, TQ), 1)
    for j in reversed(range(TQ // TK)):
        carry = block(n_full + j, carry, row + j * TK < col)
    carry = lax.fori_loop(0, n_full, lambda i, c: block(n_full - 1 - i, c, None), carry)
    o_ref[0] = carry[1]


def _attention(name, body, q_t, k, v_t, extra=()):
    b, _, s = q_t.shape
    nkt = s // TK
    in_specs = [pl.BlockSpec((1, HEAD_PAD, TQ), lambda i, h, j: (i, h, j)),
                pl.BlockSpec((1, s, HEAD_PAD), lambda i, h, j: (i, 0, h)),
                pl.BlockSpec((1, nkt, HEAD_V, TK), lambda i, h, j: (i, 0, h, 0))]
    in_specs += [pl.BlockSpec(e.shape, lambda i, h, j: (0,) * e.ndim) for e in extra]
    return pl.pallas_call(
        body,
        grid=(b, HEADS, s // TQ),
        in_specs=in_specs,
        out_specs=pl.BlockSpec((1, HEAD_V, TQ), lambda i, h, j: (i, h, j)),
        out_shape=jax.ShapeDtypeStruct((b, HEADS * HEAD_V, s), F32),
        compiler_params=pltpu.CompilerParams(dimension_semantics=("arbitrary",) * 3,
                                             vmem_limit_bytes=VMEM_LIMIT),
        name=name,
    )(q_t, k, v_t, *extra)


def _out_body(x_ref, ymla_ref, ysb_ref, yfox_ref, ymem_ref, gate_ref, mr_ref, wmu_ref, wb_ref, wo_ref,
              g_ref, b_ref, o_ref):
    mr = mr_ref[0]
    merged = None
    for n, y_ref in enumerate((ymla_ref, ysb_ref, yfox_ref, ymem_ref)):
        y = y_ref[0].T
        gz = gate_ref[0, :, n * BRANCH_WIDTH:(n + 1) * BRANCH_WIDTH].astype(F32)
        yb = (y * (gz * jax.nn.sigmoid(gz))).astype(BF16)
        branch = _dot(yb, wb_ref[n])
        mg = jax.nn.sigmoid(_dot(mr, wmu_ref[:, n * D_MODEL:(n + 1) * D_MODEL]))
        merged = mg * branch if merged is None else merged + mg * branch
    r = DEEPNORM_ALPHA * x_ref[0] + _dot(merged.astype(BF16), wo_ref[...])
    mu = jnp.mean(r, axis=-1, keepdims=True)
    var = jnp.mean(jnp.square(r - mu), axis=-1, keepdims=True)
    o_ref[0] = (r - mu) * lax.rsqrt(var + LN_EPS) * g_ref[...] + b_ref[...]


def _out(x, ys, gate, mr, wmu, wb, wo, g, bias):
    b, s, d = x.shape
    full = lambda a: pl.BlockSpec(a.shape, lambda i, j: (0,) * a.ndim)
    yt_spec = pl.BlockSpec((1, BRANCH_WIDTH, TM), lambda i, j: (i, 0, j))
    return pl.pallas_call(
        _out_body,
        grid=(b, s // TM),
        in_specs=[pl.BlockSpec((1, TM, d), lambda i, j: (i, j, 0)), yt_spec, yt_spec, yt_spec, yt_spec,
                  pl.BlockSpec((1, TM, N_BRANCHES * BRANCH_WIDTH), lambda i, j: (i, j, 0)),
                  pl.BlockSpec((1, TM, MERGE_RANK), lambda i, j: (i, j, 0)),
                  full(wmu), full(wb), full(wo), full(g), full(bias)],
        out_specs=pl.BlockSpec((1, TM, d), lambda i, j: (i, j, 0)),
        out_shape=jax.ShapeDtypeStruct((b, s, d), F32),
        compiler_params=pltpu.CompilerParams(dimension_semantics=("arbitrary", "arbitrary"),
                                             vmem_limit_bytes=VMEM_LIMIT),
        name="fused_output",
    )(x, *ys, gate, mr, wmu, wb, wo, g, bias)


def kernel(x, mem, positions, w_in, mla_q_norm, mla_w_qb, mla_kv_norm, mla_w_kvb, fox_forget_bias, w_mem_kv,
           w_merge_up, w_branch, w_out, ln_gain, ln_bias):
    b, s, d = x.shape
    assert d == D_MODEL and s % TM == 0 and s % TQ == 0 and TQ % TK == 0 and TM % TK == 0
    depth = w_in.shape[0]

    cos_t, sin_t = _rope_tables(positions)
    cos_r, sin_r = cos_t.transpose(0, 2, 1), sin_t.transpose(0, 2, 1)
    tab = jnp.concatenate([cos_r, cos_r, sin_r, sin_r, jnp.zeros((b, s, HEAD_PAD - 2 * MLA_ROPE), F32)], axis=-1)

    mkv = _memkv(mem, w_mem_kv).reshape(depth, b, MEM_LEN, 2, HEADS, HEAD_V)
    kmem = mkv[:, :, :, 0].transpose(0, 1, 3, 2, 4).astype(BF16)
    vmemt = mkv[:, :, :, 1].transpose(0, 1, 3, 4, 2).astype(BF16)

    consts = _selection_constants()
    tri = jnp.asarray(np.triu(np.ones((TK, TK), np.float32)), BF16)

    for l in range(depth):
        weights = _proj_weights(w_in[l], mla_q_norm[l], mla_w_qb[l], mla_kv_norm[l], mla_w_kvb[l],
                                fox_forget_bias[l])
        (qmla, kmla, vmla, qsb, ksb, vsb, qfox, kfox, vfox, ymem, gate, mr) = _proj(
            x, weights, consts, cos_t, sin_t, tab, kmem[l], vmemt[l])
        ymla = _attention("mla_attention", _softmax_attn_body, qmla, kmla, vmla)
        ysb = _attention("stick_breaking_attention", _sb_attn_body, qsb, ksb, vsb, extra=(tri,))
        yfox = _attention("forgetting_attention", _softmax_attn_body, qfox, kfox, vfox)
        x = _out(x, (ymla, ysb, yfox, ymem), gate, mr, w_merge_up[l].astype(BF16), w_branch[l].astype(BF16),
                 w_out[l].astype(BF16), ln_gain[l].reshape(1, d), ln_bias[l].reshape(1, d))
    return x
```

```python
import functools

import jax
import jax.numpy as jnp
import numpy as np
from jax import lax
from jax.experimental import pallas as pl
from jax.experimental.pallas import tpu as pltpu

F32, BF16 = jnp.float32, jnp.bfloat16

D_MODEL = 1024
DEPTH = 4
MEM_LEN = 256
N_BRANCHES = 4
BRANCH_WIDTH = 256
HEADS = 4
HEAD_V = 64
MLA_NOPE = 64
MLA_ROPE = 32
MLA_Q_RANK = 384
MLA_KV_RANK = 256
ROPE_THETA = 10000.0
SB_HEAD_DIM = 64
MERGE_RANK = 128
RMS_EPS = 1e-6
LN_EPS = 1e-5
DEEPNORM_ALPHA = (2 * DEPTH) ** 0.25

_C_CQ, _C_CKV, _C_KR = 0, 384, 640
_C_SQ, _C_SK, _C_SV = 672, 928, 1184
_C_FQ, _C_FK, _C_FV = 1440, 1696, 1952
_C_FF, _C_MQ, _C_GATE, _C_MR = 2208, 2212, 2468, 3492

HEAD_PAD = 128
TM = 512
TQ = 512
TK_SM = 256
TK_SB = 256
HB = 4
LOG2E = 1.4426950408889634
NEG = -1e30
VMEM_LIMIT = 56 * 1024 * 1024

_T_CQ, _T_CKV, _T_SQ, _T_SV, _T_FQ, _T_FV, _T_MQ, _T_FF, _T_END = 0, 384, 640, 896, 1152, 1408, 1664, 1920, 1936
_R_CKV, _R_ROPE, _R_SK, _R_FK, _R_GATE, _R_MR, _R_END = 0, 256, 384, 896, 1408, 2432, 2560
AUG = 16


def _dot(a, b):
    return jnp.dot(a, b, preferred_element_type=F32)


def _dot_nt(a, b):
    return lax.dot_general(a, b, (((1,), (1,)), ((), ())), preferred_element_type=F32)


def _split3(v):
    p1 = v.astype(BF16).astype(F32)
    r1 = v - p1
    p2 = r1.astype(BF16).astype(F32)
    p3 = (r1 - p2).astype(BF16).astype(F32)
    return p1, p2, p3


def _rope_body(pos_ref, invf_ref, cos_ref, sin_ref):
    ang = pos_ref[0].astype(F32) * invf_ref[...]
    cos_ref[0] = jnp.cos(ang)
    sin_ref[0] = jnp.sin(ang)


def _rope_tables(positions):
    b, s = positions.shape
    half = MLA_ROPE // 2
    inv_freq = ROPE_THETA ** (-jnp.arange(0, MLA_ROPE, 2, dtype=F32) / MLA_ROPE)
    return pl.pallas_call(
        _rope_body,
        grid=(b,),
        in_specs=[pl.BlockSpec((1, 1, s), lambda i: (i, 0, 0)),
                  pl.BlockSpec((half, 1), lambda i: (0, 0))],
        out_specs=[pl.BlockSpec((1, half, s), lambda i: (i, 0, 0))] * 2,
        out_shape=[jax.ShapeDtypeStruct((b, half, s), F32)] * 2,
        name="rope_tables",
    )(positions.reshape(b, 1, s), inv_freq.reshape(half, 1))


def _memkv_body(mem_ref, w_ref, o_ref):
    o_ref[0] = _dot(mem_ref[...].astype(BF16), w_ref[0].astype(BF16))


def _memkv(mem, w_mem_kv):
    b, m, d = mem.shape
    depth, _, n = w_mem_kv.shape
    return pl.pallas_call(
        _memkv_body,
        grid=(depth,),
        in_specs=[pl.BlockSpec((b * m, d), lambda l: (0, 0)),
                  pl.BlockSpec((1, d, n), lambda l: (l, 0, 0))],
        out_specs=pl.BlockSpec((1, b * m, n), lambda l: (l, 0, 0)),
        out_shape=jax.ShapeDtypeStruct((depth, b * m, n), F32),
        compiler_params=pltpu.CompilerParams(vmem_limit_bytes=VMEM_LIMIT),
        name="mem_kv_proj",
    )(mem.reshape(b * m, d), w_mem_kv)


def _proj_body(x_ref, wt_ref, wr_ref, qn_ref, kvnc_ref, kvnr_ref, wqbt_ref, wvt_ref, wkmla_ref, p2_ref,
               cos_ref, sin_ref, tab_ref, fbias_ref, selq_ref, onesq_ref, selk_ref, onesk_ref,
               kmem_ref, vmemt_ref,
               qmla_o, kmla_o, vmla_o, qsb_o, ksb_o, vsb_o, qfox_o, kfox_o, vfox_o, ymem_o, gate_o, mr_o,
               carry_ref):
    tm = x_ref.shape[1]
    xb = x_ref[0].astype(BF16)

    def store_vt(o_ref, vt):
        tk = o_ref.shape[3]
        for j in range(tm // tk):
            o_ref[0, j] = vt[:, j * tk:(j + 1) * tk].astype(BF16)

    cq = _dot_nt(wt_ref[_T_CQ:_T_CKV, :], xb)
    cqn = cq * lax.rsqrt(jnp.mean(cq * cq, axis=0, keepdims=True) + RMS_EPS) * qn_ref[...]
    q = _dot(wqbt_ref[...], cqn.astype(BF16)) * ((MLA_NOPE + MLA_ROPE) ** -0.5 * LOG2E)
    cos_t, sin_t = cos_ref[0], sin_ref[0]
    half = MLA_ROPE // 2
    dq = MLA_NOPE + MLA_ROPE
    zq = jnp.zeros((HEAD_PAD - dq, tm), F32)
    for h in range(HEADS):
        nope = q[h * dq:h * dq + MLA_NOPE]
        x1 = q[h * dq + MLA_NOPE:h * dq + MLA_NOPE + half]
        x2 = q[h * dq + MLA_NOPE + half:(h + 1) * dq]
        blk = jnp.concatenate([nope, x1 * cos_t - x2 * sin_t, x1 * sin_t + x2 * cos_t, zq], axis=0)
        qmla_o[0, h * HEAD_PAD:(h + 1) * HEAD_PAD, :] = blk.astype(BF16)

    ckv_t = _dot_nt(wt_ref[_T_CKV:_T_SQ, :], xb)
    ckvn_t = ckv_t * lax.rsqrt(jnp.mean(ckv_t * ckv_t, axis=0, keepdims=True) + RMS_EPS) * kvnc_ref[...]
    store_vt(vmla_o, _dot(wvt_ref[...], ckvn_t.astype(BF16)))
    ckv_r = _dot(xb, wr_ref[:, _R_CKV:_R_ROPE])
    ckvn_r = ckv_r * lax.rsqrt(jnp.mean(ckv_r * ckv_r, axis=1, keepdims=True) + RMS_EPS) * kvnr_ref[...]
    prod = _dot(xb, wr_ref[:, _R_ROPE:_R_SK]) * tab_ref[0]
    kpe = prod + pltpu.roll(prod, 128 - MLA_ROPE, 1)
    kmla = _dot(ckvn_r.astype(BF16), wkmla_ref[...]) + _dot(kpe.astype(BF16), p2_ref[...])
    kmla_o[0] = kmla.astype(BF16)

    sq = _dot_nt(wt_ref[_T_SQ:_T_SV, :], xb) * (SB_HEAD_DIM ** -0.5 * LOG2E)
    zs = jnp.zeros((HEAD_PAD - SB_HEAD_DIM, tm), F32)
    for h in range(HEADS):
        blk = jnp.concatenate([sq[h * SB_HEAD_DIM:(h + 1) * SB_HEAD_DIM], zs], axis=0)
        qsb_o[0, h * HEAD_PAD:(h + 1) * HEAD_PAD, :] = blk.astype(BF16)
    store_vt(vsb_o, _dot_nt(wt_ref[_T_SV:_T_FQ, :], xb))
    ksb_o[0] = _dot(xb, wr_ref[:, _R_SK:_R_FK]).astype(BF16)

    ff = _dot_nt(wt_ref[_T_FF:_T_END, :], xb) + fbias_ref[...]
    lf = jnp.minimum(ff, 0.0) - jnp.log1p(jnp.exp(-jnp.abs(ff)))
    l1, l2, l3 = _split3(lf)
    upper = (lax.broadcasted_iota(jnp.int32, (tm, tm), 0) <= lax.broadcasted_iota(jnp.int32, (tm, tm), 1))
    csum = _dot(jnp.concatenate([l1, l2, l3], axis=0).astype(BF16), upper.astype(BF16))
    @pl.when(pl.program_id(1) == 0)
    def _():
        carry_ref[...] = jnp.zeros_like(carry_ref)

    c = csum[0:AUG] + csum[AUG:2 * AUG] + csum[2 * AUG:3 * AUG] + carry_ref[:, 0:1]
    carry_ref[...] = jnp.broadcast_to(c[:, tm - 1:tm], carry_ref.shape)
    c1, c2, c3 = _split3(c * LOG2E)
    cst = jnp.concatenate([c1, c2, c3, jnp.zeros((HEAD_PAD - 3 * AUG, tm), F32)], axis=0)
    aug_t = _dot(selq_ref[...], cst.astype(BF16)) + onesq_ref[...]
    fq = _dot_nt(wt_ref[_T_FQ:_T_FV, :], xb) * (SB_HEAD_DIM ** -0.5 * LOG2E)
    zf = jnp.zeros((HEAD_PAD - SB_HEAD_DIM - 8, tm), F32)
    for h in range(HEADS):
        blk = jnp.concatenate([fq[h * SB_HEAD_DIM:(h + 1) * SB_HEAD_DIM], aug_t[h * 8:(h + 1) * 8], zf], axis=0)
        qfox_o[0, h * HEAD_PAD:(h + 1) * HEAD_PAD, :] = blk.astype(BF16)
    store_vt(vfox_o, _dot_nt(wt_ref[_T_FV:_T_MQ, :], xb))
    kfox = _dot(xb, wr_ref[:, _R_FK:_R_GATE]) + _dot(cst.T.astype(BF16), selk_ref[...]) + onesk_ref[...]
    kfox_o[0] = kfox.astype(BF16)

    mq = _dot_nt(wt_ref[_T_MQ:_T_FF, :], xb) * (HEAD_V ** -0.5)
    for h in range(HEADS):
        s = _dot(kmem_ref[0, h], mq[h * HEAD_V:(h + 1) * HEAD_V].astype(BF16))
        p = jnp.exp(s - jnp.max(s, axis=0, keepdims=True))
        o = _dot(vmemt_ref[0, h], p.astype(BF16))
        ymem_o[0, h * HEAD_V:(h + 1) * HEAD_V, :] = o / jnp.sum(p, axis=0, keepdims=True)

    gate_o[0] = _dot(xb, wr_ref[:, _R_GATE:_R_MR]).astype(BF16)
    mr_o[0] = _dot(xb, wr_ref[:, _R_MR:_R_END]).astype(BF16)


def _placed(w, n_in):
    d = w.shape[0]
    w = w.reshape(d, HEADS, n_in)
    return jnp.pad(w, ((0, 0), (0, 0), (0, HEAD_PAD - n_in))).reshape(d, HEADS * HEAD_PAD)


def _selection_constants():
    selq = np.zeros((HEADS * 8, HEAD_PAD), np.float32)
    onesq = np.zeros((HEADS * 8, 1), np.float32)
    selk = np.zeros((HEAD_PAD, HEADS * HEAD_PAD), np.float32)
    onesk = np.zeros((1, HEADS * HEAD_PAD), np.float32)
    p2 = np.zeros((HEAD_PAD, HEADS * HEAD_PAD), np.float32)
    for h in range(HEADS):
        for i in range(3):
            selq[h * 8 + i, i * AUG + h] = 1.0
            onesq[h * 8 + 3 + i, 0] = 1.0
            onesk[0, h * HEAD_PAD + SB_HEAD_DIM + i] = 1.0
            selk[i * AUG + h, h * HEAD_PAD + SB_HEAD_DIM + 3 + i] = -1.0
        for j in range(MLA_ROPE):
            p2[j, h * HEAD_PAD + MLA_NOPE + j] = 1.0
    return (jnp.asarray(selq, BF16), jnp.asarray(onesq), jnp.asarray(selk, BF16), jnp.asarray(onesk),
            jnp.asarray(p2, BF16))


def _proj_weights(w_in, q_norm, w_qb, kv_norm, w_kvb, fox_bias):
    cols = lambda a, n: w_in[:, a:a + n]
    wt = jnp.concatenate([
        cols(_C_CQ, 384).T, cols(_C_CKV, 256).T, cols(_C_SQ, 256).T, cols(_C_SV, 256).T,
        cols(_C_FQ, 256).T, cols(_C_FV, 256).T, cols(_C_MQ, 256).T, cols(_C_FF, 4).T,
        jnp.zeros((_T_END - _T_FF - HEADS, D_MODEL), F32)], axis=0).astype(BF16)
    kr = cols(_C_KR, MLA_ROPE)
    half = MLA_ROPE // 2
    rope = jnp.concatenate([kr, -kr[:, half:], kr[:, :half], jnp.zeros((D_MODEL, HEAD_PAD - 2 * MLA_ROPE), F32)], axis=1)
    wr = jnp.concatenate([
        cols(_C_CKV, 256), rope, _placed(cols(_C_SK, 256), SB_HEAD_DIM), _placed(cols(_C_FK, 256), SB_HEAD_DIM),
        cols(_C_GATE, 1024), cols(_C_MR, MERGE_RANK)], axis=1).astype(BF16)
    kvb = w_kvb.reshape(MLA_KV_RANK, HEADS, MLA_NOPE + HEAD_V)
    wkmla = jnp.pad(kvb[:, :, :MLA_NOPE], ((0, 0), (0, 0), (0, HEAD_PAD - MLA_NOPE))).reshape(MLA_KV_RANK, HEADS * HEAD_PAD)
    wvt = kvb[:, :, MLA_NOPE:].reshape(MLA_KV_RANK, HEADS * HEAD_V).T
    fbias = jnp.pad(fox_bias, (0, AUG - HEADS)).reshape(AUG, 1)
    return (wt, wr, q_norm.reshape(-1, 1), kv_norm.reshape(-1, 1), kv_norm.reshape(1, -1),
            w_qb.T.astype(BF16), wvt.astype(BF16), wkmla.astype(BF16), fbias)


def _proj(x, weights, consts, cos_t, sin_t, tab, kmem, vmemt):
    b, s, d = x.shape
    wt, wr, qn, kvnc, kvnr, wqbt, wvt, wkmla, fbias = weights
    selq, onesq, selk, onesk, p2 = consts
    full = lambda a: pl.BlockSpec(a.shape, lambda i, j: (0,) * a.ndim)
    qt_spec = pl.BlockSpec((1, HEADS * HEAD_PAD, TM), lambda i, j: (i, 0, j))
    k_spec = pl.BlockSpec((1, TM, HEADS * HEAD_PAD), lambda i, j: (i, j, 0))
    vt_spec = lambda tk: pl.BlockSpec((1, TM // tk, HEADS * HEAD_V, tk), lambda i, j: (i, j, 0, 0))
    qt_shape = jax.ShapeDtypeStruct((b, HEADS * HEAD_PAD, s), BF16)
    k_shape = jax.ShapeDtypeStruct((b, s, HEADS * HEAD_PAD), BF16)
    vt_shape = lambda tk: jax.ShapeDtypeStruct((b, s // tk, HEADS * HEAD_V, tk), BF16)
    return pl.pallas_call(
        _proj_body,
        grid=(b, s // TM),
        in_specs=[pl.BlockSpec((1, TM, d), lambda i, j: (i, j, 0)),
                  full(wt), full(wr), full(qn), full(kvnc), full(kvnr), full(wqbt), full(wvt), full(wkmla), full(p2),
                  pl.BlockSpec((1, MLA_ROPE // 2, TM), lambda i, j: (i, 0, j)),
                  pl.BlockSpec((1, MLA_ROPE // 2, TM), lambda i, j: (i, 0, j)),
                  pl.BlockSpec((1, TM, HEAD_PAD), lambda i, j: (i, j, 0)),
                  full(fbias), full(selq), full(onesq), full(selk), full(onesk),
                  pl.BlockSpec((1, HEADS, MEM_LEN, HEAD_V), lambda i, j: (i, 0, 0, 0)),
                  pl.BlockSpec((1, HEADS, HEAD_V, MEM_LEN), lambda i, j: (i, 0, 0, 0))],
        out_specs=[qt_spec, k_spec, vt_spec(TK_SM), qt_spec, k_spec, vt_spec(TK_SB), qt_spec, k_spec, vt_spec(TK_SM),
                   pl.BlockSpec((1, HEADS * HEAD_V, TM), lambda i, j: (i, 0, j)),
                   pl.BlockSpec((1, TM, N_BRANCHES * BRANCH_WIDTH), lambda i, j: (i, j, 0)),
                   pl.BlockSpec((1, TM, MERGE_RANK), lambda i, j: (i, j, 0))],
        out_shape=[qt_shape, k_shape, vt_shape(TK_SM), qt_shape, k_shape, vt_shape(TK_SB), qt_shape, k_shape,
                   vt_shape(TK_SM),
                   jax.ShapeDtypeStruct((b, HEADS * HEAD_V, s), F32),
                   jax.ShapeDtypeStruct((b, s, N_BRANCHES * BRANCH_WIDTH), BF16),
                   jax.ShapeDtypeStruct((b, s, MERGE_RANK), BF16)],
        scratch_shapes=[pltpu.VMEM((AUG, 128), F32)],
        compiler_params=pltpu.CompilerParams(dimension_semantics=("arbitrary", "arbitrary"),
                                             vmem_limit_bytes=VMEM_LIMIT),
        name="fused_projection",
    )(x, wt, wr, qn, kvnc, kvnr, wqbt, wvt, wkmla, p2, cos_t, sin_t, tab, fbias, selq, onesq, selk, onesk,
      kmem, vmemt)


def _causal_masks(tq, tk, strict):
    row = lax.broadcasted_iota(jnp.int32, (tk, tq), 0)
    col = lax.broadcasted_iota(jnp.int32, (tk, tq), 1)
    return [(row + j * tk < col) if strict else (row + j * tk <= col) for j in range(tq // tk)]


def _softmax_attn_body(q_ref, k_ref, v_ref, o_ref, acc_ref, s_ref, *, tq, tk, hb):
    nd = tq // tk
    qi = pl.program_id(2)
    n_full = qi * nd
    acc_ref[...] = jnp.zeros_like(acc_ref)

    def scores(k, h):
        rows = pl.ds(pl.multiple_of(k * tk, tk), tk)
        return _dot(k_ref[0, rows, h * HEAD_PAD:(h + 1) * HEAD_PAD], q_ref[0, h * HEAD_PAD:(h + 1) * HEAD_PAD, :])

    def step(k, slot, k_next, carry, mask):
        out = [None] * hb

        def prefetch(h):
            if k_next is not None and h < hb:
                s_ref[1 - slot, h] = scores(k_next, h)

        def softmax_pv(h):
            m, l = carry[h]
            s = s_ref[slot, h]
            if mask is not None:
                s = jnp.where(mask, s, NEG)
            m_new = jnp.maximum(m, jnp.max(s, axis=0, keepdims=True))
            alpha = jnp.exp2(m - m_new)
            p = jnp.exp2(s - m_new)
            if mask is not None:
                p = jnp.where(mask, p, 0.0)
            l = alpha * l + jnp.sum(p, axis=0, keepdims=True)
            hv = slice(h * HEAD_V, (h + 1) * HEAD_V)
            acc_ref[hv, :] = alpha * acc_ref[hv, :] + _dot(v_ref[0, k, hv, :], p.astype(BF16))
            out[h] = (m_new, l)

        prefetch(0)
        for h in range(hb):
            prefetch(h + 1)
            softmax_pv(h)
        return tuple(out)

    for h in range(hb):
        s_ref[0, h] = scores(0, h)

    def pair(i, carry):
        carry = step(2 * i, 0, 2 * i + 1, carry, None)
        return step(2 * i + 1, 1, 2 * i + 2, carry, None)

    carry = tuple((jnp.full((1, tq), NEG, F32), jnp.zeros((1, tq), F32)) for _ in range(hb))
    carry = lax.fori_loop(0, n_full // 2, pair, carry)
    masks = _causal_masks(tq, tk, strict=False)
    for j in range(nd):
        carry = step(n_full + j, j % 2, n_full + j + 1 if j + 1 < nd else None, carry, masks[j])
    for h in range(hb):
        hv = slice(h * HEAD_V, (h + 1) * HEAD_V)
        o_ref[0, hv, :] = acc_ref[hv, :] * (1.0 / carry[h][1])


def _sb_attn_body(q_ref, k_ref, v_ref, ntri_ref, o_ref, acc_ref, s_ref, *, tq, tk, hb):
    nd = tq // tk
    qi = pl.program_id(2)
    n_full = qi * nd
    acc_ref[...] = jnp.zeros_like(acc_ref)

    def scores(k, h):
        rows = pl.ds(pl.multiple_of(k * tk, tk), tk)
        return _dot(k_ref[0, rows, h * HEAD_PAD:(h + 1) * HEAD_PAD], q_ref[0, h * HEAD_PAD:(h + 1) * HEAD_PAD, :])

    def step(k, slot, carry, mask):
        k_next = jnp.maximum(k - 1, 0)
        out = [None] * hb
        local = [None] * hb

        def suffix(h):
            if h >= hb:
                return
            z = s_ref[slot, h]
            neg_abs = pltpu.bitcast(pltpu.bitcast(z, jnp.uint32) | jnp.uint32(0x80000000), F32)
            sp = jnp.maximum(z, 0.0) + jnp.log(1.0 + jnp.exp2(neg_abs)) * LOG2E
            if mask is not None:
                sp = jnp.where(mask, sp, 0.0)
            local[h] = _dot(ntri_ref[...], sp.astype(BF16))
            s_ref[1 - slot, h] = scores(k_next, h)

        def weights_pv(h):
            cross = carry[h]
            a = jnp.exp2(s_ref[slot, h] + local[h])
            if mask is not None:
                a = jnp.where(mask, a, 0.0)
            hv = slice(h * HEAD_V, (h + 1) * HEAD_V)
            acc_ref[hv, :] += _dot(v_ref[0, k, hv, :], a.astype(BF16)) * jnp.exp2(cross)
            out[h] = cross + local[h][0:1, :]

        suffix(0)
        for h in range(hb):
            suffix(h + 1)
            weights_pv(h)
        return tuple(out)

    for h in range(hb):
        s_ref[(nd - 1) % 2, h] = scores(n_full + nd - 1, h)
    carry = tuple(jnp.zeros((1, tq), F32) for _ in range(hb))
    masks = _causal_masks(tq, tk, strict=True)
    for j in reversed(range(nd)):
        carry = step(n_full + j, j % 2, carry, masks[j])

    def pair(i, carry):
        k = n_full - 1 - 2 * i
        carry = step(k, 1, carry, None)
        return step(k - 1, 0, carry, None)

    lax.fori_loop(0, n_full // 2, pair, carry)
    o_ref[0] = acc_ref[...]


def _attention(name, body, q_t, k, v_t, tk, hb, extra=()):
    b, _, s = q_t.shape
    in_specs = [pl.BlockSpec((1, hb * HEAD_PAD, TQ), lambda i, g, j: (i, g, j)),
                pl.BlockSpec((1, s, hb * HEAD_PAD), lambda i, g, j: (i, 0, g)),
                pl.BlockSpec((1, s // tk, hb * HEAD_V, tk), lambda i, g, j: (i, 0, g, 0))]
    in_specs += [pl.BlockSpec(e.shape, lambda i, g, j: (0,) * e.ndim) for e in extra]
    return pl.pallas_call(
        functools.partial(body, tq=TQ, tk=tk, hb=hb),
        grid=(b, HEADS // hb, s // TQ),
        in_specs=in_specs,
        out_specs=pl.BlockSpec((1, hb * HEAD_V, TQ), lambda i, g, j: (i, g, j)),
        out_shape=jax.ShapeDtypeStruct((b, HEADS * HEAD_V, s), F32),
        scratch_shapes=[pltpu.VMEM((hb * HEAD_V, TQ), F32), pltpu.VMEM((2, hb, tk, TQ), F32)],
        compiler_params=pltpu.CompilerParams(dimension_semantics=("arbitrary",) * 3,
                                             vmem_limit_bytes=VMEM_LIMIT),
        name=name,
    )(q_t, k, v_t, *extra)


def _out_body(x_ref, ymla_ref, ysb_ref, yfox_ref, ymem_ref, gate_ref, mr_ref, wmu_ref, wb_ref, wo_ref,
              g_ref, b_ref, o_ref):
    mr = mr_ref[0]
    merged = None
    for n, y_ref in enumerate((ymla_ref, ysb_ref, yfox_ref, ymem_ref)):
        y = y_ref[0].T
        gz = gate_ref[0, :, n * BRANCH_WIDTH:(n + 1) * BRANCH_WIDTH].astype(F32)
        yb = (y * (gz * jax.nn.sigmoid(gz))).astype(BF16)
        branch = _dot(yb, wb_ref[n])
        mg = jax.nn.sigmoid(_dot(mr, wmu_ref[:, n * D_MODEL:(n + 1) * D_MODEL]))
        merged = mg * branch if merged is None else merged + mg * branch
    r = DEEPNORM_ALPHA * x_ref[0] + _dot(merged.astype(BF16), wo_ref[...])
    mu = jnp.mean(r, axis=-1, keepdims=True)
    var = jnp.mean(jnp.square(r - mu), axis=-1, keepdims=True)
    o_ref[0] = (r - mu) * lax.rsqrt(var + LN_EPS) * g_ref[...] + b_ref[...]


def _out(x, ys, gate, mr, wmu, wb, wo, g, bias):
    b, s, d = x.shape
    full = lambda a: pl.BlockSpec(a.shape, lambda i, j: (0,) * a.ndim)
    yt_spec = pl.BlockSpec((1, BRANCH_WIDTH, TM), lambda i, j: (i, 0, j))
    return pl.pallas_call(
        _out_body,
        grid=(b, s // TM),
        in_specs=[pl.BlockSpec((1, TM, d), lambda i, j: (i, j, 0)), yt_spec, yt_spec, yt_spec, yt_spec,
                  pl.BlockSpec((1, TM, N_BRANCHES * BRANCH_WIDTH), lambda i, j: (i, j, 0)),
                  pl.BlockSpec((1, TM, MERGE_RANK), lambda i, j: (i, j, 0)),
                  full(wmu), full(wb), full(wo), full(g), full(bias)],
        out_specs=pl.BlockSpec((1, TM, d), lambda i, j: (i, j, 0)),
        out_shape=jax.ShapeDtypeStruct((b, s, d), F32),
        compiler_params=pltpu.CompilerParams(dimension_semantics=("arbitrary", "arbitrary"),
                                             vmem_limit_bytes=VMEM_LIMIT),
        name="fused_output",
    )(x, *ys, gate, mr, wmu, wb, wo, g, bias)


def kernel(x, mem, positions, w_in, mla_q_norm, mla_w_qb, mla_kv_norm, mla_w_kvb, fox_forget_bias, w_mem_kv,
           w_merge_up, w_branch, w_out, ln_gain, ln_bias):
    b, s, d = x.shape
    assert d == D_MODEL and s % TM == 0 and s % TQ == 0
    assert all(TQ % (2 * tk) == 0 and TM % tk == 0 for tk in (TK_SM, TK_SB)) and HEADS % HB == 0
    depth = w_in.shape[0]

    cos_t, sin_t = _rope_tables(positions)
    cos_r, sin_r = cos_t.transpose(0, 2, 1), sin_t.transpose(0, 2, 1)
    tab = jnp.concatenate([cos_r, cos_r, sin_r, sin_r, jnp.zeros((b, s, HEAD_PAD - 2 * MLA_ROPE), F32)], axis=-1)

    mkv = _memkv(mem, w_mem_kv).reshape(depth, b, MEM_LEN, 2, HEADS, HEAD_V)
    kmem = mkv[:, :, :, 0].transpose(0, 1, 3, 2, 4).astype(BF16)
    vmemt = mkv[:, :, :, 1].transpose(0, 1, 3, 4, 2).astype(BF16)

    consts = _selection_constants()
    ntri = jnp.asarray(-np.triu(np.ones((TK_SB, TK_SB), np.float32)), BF16)

    for l in range(depth):
        weights = _proj_weights(w_in[l], mla_q_norm[l], mla_w_qb[l], mla_kv_norm[l], mla_w_kvb[l],
                                fox_forget_bias[l])
        (qmla, kmla, vmla, qsb, ksb, vsb, qfox, kfox, vfox, ymem, gate, mr) = _proj(
            x, weights, consts, cos_t, sin_t, tab, kmem[l], vmemt[l])
        ymla = _attention("mla_attention", _softmax_attn_body, qmla, kmla, vmla, TK_SM, HB)
        ysb = _attention("stick_breaking_attention", _sb_attn_body, qsb, ksb, vsb, TK_SB, HB, extra=(ntri,))
        yfox = _attention("forgetting_attention", _softmax_attn_body, qfox, kfox, vfox, TK_SM, HB)
        x = _out(x, (ymla, ysb, yfox, ymem), gate, mr, w_merge_up[l].astype(BF16), w_branch[l].astype(BF16),
                 w_out[l].astype(BF16), ln_gain[l].reshape(1, d), ln_bias[l].reshape(1, d))
    return x
```

```python
import functools

import jax
import jax.numpy as jnp
import numpy as np
from jax import lax
from jax.experimental import pallas as pl
from jax.experimental.pallas import tpu as pltpu

F32, BF16 = jnp.float32, jnp.bfloat16

D_MODEL = 1024
DEPTH = 4
MEM_LEN = 256
N_BRANCHES = 4
BRANCH_WIDTH = 256
HEADS = 4
HEAD_V = 64
MLA_NOPE = 64
MLA_ROPE = 32
MLA_Q_RANK = 384
MLA_KV_RANK = 256
ROPE_THETA = 10000.0
SB_HEAD_DIM = 64
MERGE_RANK = 128
RMS_EPS = 1e-6
LN_EPS = 1e-5
DEEPNORM_ALPHA = (2 * DEPTH) ** 0.25

_C_CQ, _C_CKV, _C_KR = 0, 384, 640
_C_SQ, _C_SK, _C_SV = 672, 928, 1184
_C_FQ, _C_FK, _C_FV = 1440, 1696, 1952
_C_FF, _C_MQ, _C_GATE, _C_MR = 2208, 2212, 2468, 3492

HEAD_PAD = 128
TM = 512
TQ = 512
TK_SM = 256
TK_SB = 256
HB = 4
OUT_SPLIT = 4
LOG2E = 1.4426950408889634
SB_DEAD_LOG2 = -160.0
NEG = -1e30
VMEM_LIMIT = 56 * 1024 * 1024

_T_CQ, _T_CKV, _T_SQ, _T_SV, _T_FQ, _T_FV, _T_MQ, _T_FF, _T_END = 0, 384, 640, 896, 1152, 1408, 1664, 1920, 1936
_R_CKV, _R_ROPE, _R_SK, _R_FK, _R_GATE, _R_MR, _R_END = 0, 256, 384, 896, 1408, 2432, 2560
AUG = 16


def _dot(a, b):
    return jnp.dot(a, b, preferred_element_type=F32)


def _dot_nt(a, b):
    return lax.dot_general(a, b, (((1,), (1,)), ((), ())), preferred_element_type=F32)


def _sigmoid(v):
    return 0.5 * jnp.tanh(0.5 * v) + 0.5


def _split3(v):
    p1 = v.astype(BF16).astype(F32)
    r1 = v - p1
    p2 = r1.astype(BF16).astype(F32)
    p3 = (r1 - p2).astype(BF16).astype(F32)
    return p1, p2, p3


def _rope_body(pos_ref, invf_ref, cos_ref, sin_ref):
    ang = pos_ref[0].astype(F32) * invf_ref[...]
    cos_ref[0] = jnp.cos(ang)
    sin_ref[0] = jnp.sin(ang)


def _rope_tables(positions):
    b, s = positions.shape
    half = MLA_ROPE // 2
    inv_freq = ROPE_THETA ** (-jnp.arange(0, MLA_ROPE, 2, dtype=F32) / MLA_ROPE)
    return pl.pallas_call(
        _rope_body,
        grid=(b,),
        in_specs=[pl.BlockSpec((1, 1, s), lambda i: (i, 0, 0)),
                  pl.BlockSpec((half, 1), lambda i: (0, 0))],
        out_specs=[pl.BlockSpec((1, half, s), lambda i: (i, 0, 0))] * 2,
        out_shape=[jax.ShapeDtypeStruct((b, half, s), F32)] * 2,
        name="rope_tables",
    )(positions.reshape(b, 1, s), inv_freq.reshape(half, 1))


def _memkv_body(mem_ref, w_ref, o_ref):
    o_ref[0] = _dot(mem_ref[...].astype(BF16), w_ref[0].astype(BF16))


def _memkv(mem, w_mem_kv):
    b, m, d = mem.shape
    depth, _, n = w_mem_kv.shape
    return pl.pallas_call(
        _memkv_body,
        grid=(depth,),
        in_specs=[pl.BlockSpec((b * m, d), lambda l: (0, 0)),
                  pl.BlockSpec((1, d, n), lambda l: (l, 0, 0))],
        out_specs=pl.BlockSpec((1, b * m, n), lambda l: (l, 0, 0)),
        out_shape=jax.ShapeDtypeStruct((depth, b * m, n), F32),
        compiler_params=pltpu.CompilerParams(vmem_limit_bytes=VMEM_LIMIT),
        name="mem_kv_proj",
    )(mem.reshape(b * m, d), w_mem_kv)


def _proj_body(x_ref, wt_ref, wr_ref, qn_ref, kvnc_ref, kvnr_ref, wqbt_ref, wvt_ref, wkmla_ref, p2_ref,
               cos_ref, sin_ref, tab_ref, fbias_ref, selq_ref, onesq_ref, selk_ref, onesk_ref,
               kmem_ref, vmemt_ref,
               qmla_o, kmla_o, vmla_o, qsb_o, ksb_o, vsb_o, qfox_o, kfox_o, vfox_o, ymem_o, gate_o, mr_o,
               carry_ref):
    tm = x_ref.shape[1]
    xb = x_ref[0].astype(BF16)

    def store_vt(o_ref, vt):
        tk = o_ref.shape[3]
        for j in range(tm // tk):
            o_ref[0, j] = vt[:, j * tk:(j + 1) * tk].astype(BF16)

    def rms(v, axis, g):
        return (v * lax.rsqrt(jnp.mean(v * v, axis=axis, keepdims=True) + RMS_EPS) * g).astype(BF16)

    ff = _dot_nt(wt_ref[_T_FF:_T_END, :], xb) + fbias_ref[...]
    lf = jnp.minimum(ff, 0.0) - jnp.log1p(jnp.exp(-jnp.abs(ff)))
    upper = (lax.broadcasted_iota(jnp.int32, (tm, tm), 0) <= lax.broadcasted_iota(jnp.int32, (tm, tm), 1))
    cqn = rms(_dot_nt(wt_ref[_T_CQ:_T_CKV, :], xb), 0, qn_ref[...])
    ckvn_t = rms(_dot_nt(wt_ref[_T_CKV:_T_SQ, :], xb), 0, kvnc_ref[...])
    ckvn_r = rms(_dot(xb, wr_ref[:, _R_CKV:_R_ROPE]), 1, kvnr_ref[...])
    prod = _dot(xb, wr_ref[:, _R_ROPE:_R_SK]) * tab_ref[0]
    kpe = (prod + pltpu.roll(prod, 128 - MLA_ROPE, 1)).astype(BF16)
    mq = (_dot_nt(wt_ref[_T_MQ:_T_FF, :], xb) * (HEAD_V ** -0.5)).astype(BF16)

    csum = _dot(jnp.concatenate(_split3(lf), axis=0).astype(BF16), upper.astype(BF16))

    @pl.when(pl.program_id(1) == 0)
    def _():
        carry_ref[...] = jnp.zeros_like(carry_ref)

    c = csum[0:AUG] + csum[AUG:2 * AUG] + csum[2 * AUG:3 * AUG] + carry_ref[:, 0:1]
    carry_ref[...] = jnp.broadcast_to(c[:, tm - 1:tm], carry_ref.shape)
    cst = jnp.concatenate([*_split3(c * LOG2E), jnp.zeros((HEAD_PAD - 3 * AUG, tm), F32)], axis=0)

    sq = _dot_nt(wt_ref[_T_SQ:_T_SV, :], xb) * (SB_HEAD_DIM ** -0.5 * LOG2E)
    zs = jnp.zeros((HEAD_PAD - SB_HEAD_DIM, tm), F32)
    for h in range(HEADS):
        blk = jnp.concatenate([sq[h * SB_HEAD_DIM:(h + 1) * SB_HEAD_DIM], zs], axis=0)
        qsb_o[0, h * HEAD_PAD:(h + 1) * HEAD_PAD, :] = blk.astype(BF16)
    q = _dot(wqbt_ref[...], cqn) * ((MLA_NOPE + MLA_ROPE) ** -0.5 * LOG2E)
    cos_t, sin_t = cos_ref[0], sin_ref[0]
    half = MLA_ROPE // 2
    dq = MLA_NOPE + MLA_ROPE
    zq = jnp.zeros((HEAD_PAD - dq, tm), F32)
    for h in range(HEADS):
        nope = q[h * dq:h * dq + MLA_NOPE]
        x1 = q[h * dq + MLA_NOPE:h * dq + MLA_NOPE + half]
        x2 = q[h * dq + MLA_NOPE + half:(h + 1) * dq]
        blk = jnp.concatenate([nope, x1 * cos_t - x2 * sin_t, x1 * sin_t + x2 * cos_t, zq], axis=0)
        qmla_o[0, h * HEAD_PAD:(h + 1) * HEAD_PAD, :] = blk.astype(BF16)

    store_vt(vsb_o, _dot_nt(wt_ref[_T_SV:_T_FQ, :], xb))
    store_vt(vmla_o, _dot(wvt_ref[...], ckvn_t))
    ksb_o[0] = _dot(xb, wr_ref[:, _R_SK:_R_FK]).astype(BF16)
    kmla_o[0] = (_dot(ckvn_r, wkmla_ref[...]) + _dot(kpe, p2_ref[...])).astype(BF16)

    fq = _dot_nt(wt_ref[_T_FQ:_T_FV, :], xb) * (SB_HEAD_DIM ** -0.5 * LOG2E)
    aug_t = _dot(selq_ref[...], cst.astype(BF16)) + onesq_ref[...]
    zf = jnp.zeros((HEAD_PAD - SB_HEAD_DIM - 8, tm), F32)
    for h in range(HEADS):
        blk = jnp.concatenate([fq[h * SB_HEAD_DIM:(h + 1) * SB_HEAD_DIM], aug_t[h * 8:(h + 1) * 8], zf], axis=0)
        qfox_o[0, h * HEAD_PAD:(h + 1) * HEAD_PAD, :] = blk.astype(BF16)
    store_vt(vfox_o, _dot_nt(wt_ref[_T_FV:_T_MQ, :], xb))
    kfox = _dot(xb, wr_ref[:, _R_FK:_R_GATE]) + _dot(cst.T.astype(BF16), selk_ref[...]) + onesk_ref[...]
    kfox_o[0] = kfox.astype(BF16)

    s_mem = [_dot(kmem_ref[0, h], mq[h * HEAD_V:(h + 1) * HEAD_V]) for h in range(HEADS)]
    gate_o[0] = _dot(xb, wr_ref[:, _R_GATE:_R_MR]).astype(BF16)
    mr_o[0] = _dot(xb, wr_ref[:, _R_MR:_R_END]).astype(BF16)
    for h in range(HEADS):
        p = jnp.exp(s_mem[h] - jnp.max(s_mem[h], axis=0, keepdims=True))
        o = _dot(vmemt_ref[0, h], p.astype(BF16))
        ymem_o[0, h * HEAD_V:(h + 1) * HEAD_V, :] = o / jnp.sum(p, axis=0, keepdims=True)


def _placed(w, n_in):
    d = w.shape[0]
    w = w.reshape(d, HEADS, n_in)
    return jnp.pad(w, ((0, 0), (0, 0), (0, HEAD_PAD - n_in))).reshape(d, HEADS * HEAD_PAD)


def _selection_constants():
    selq = np.zeros((HEADS * 8, HEAD_PAD), np.float32)
    onesq = np.zeros((HEADS * 8, 1), np.float32)
    selk = np.zeros((HEAD_PAD, HEADS * HEAD_PAD), np.float32)
    onesk = np.zeros((1, HEADS * HEAD_PAD), np.float32)
    p2 = np.zeros((HEAD_PAD, HEADS * HEAD_PAD), np.float32)
    for h in range(HEADS):
        for i in range(3):
            selq[h * 8 + i, i * AUG + h] = 1.0
            onesq[h * 8 + 3 + i, 0] = 1.0
            onesk[0, h * HEAD_PAD + SB_HEAD_DIM + i] = 1.0
            selk[i * AUG + h, h * HEAD_PAD + SB_HEAD_DIM + 3 + i] = -1.0
        for j in range(MLA_ROPE):
            p2[j, h * HEAD_PAD + MLA_NOPE + j] = 1.0
    return (jnp.asarray(selq, BF16), jnp.asarray(onesq), jnp.asarray(selk, BF16), jnp.asarray(onesk),
            jnp.asarray(p2, BF16))


def _proj_weights(w_in, q_norm, w_qb, kv_norm, w_kvb, fox_bias):
    cols = lambda a, n: w_in[:, a:a + n]
    wt = jnp.concatenate([
        cols(_C_CQ, 384).T, cols(_C_CKV, 256).T, cols(_C_SQ, 256).T, cols(_C_SV, 256).T,
        cols(_C_FQ, 256).T, cols(_C_FV, 256).T, cols(_C_MQ, 256).T, cols(_C_FF, 4).T,
        jnp.zeros((_T_END - _T_FF - HEADS, D_MODEL), F32)], axis=0).astype(BF16)
    kr = cols(_C_KR, MLA_ROPE)
    half = MLA_ROPE // 2
    rope = jnp.concatenate([kr, -kr[:, half:], kr[:, :half], jnp.zeros((D_MODEL, HEAD_PAD - 2 * MLA_ROPE), F32)], axis=1)
    wr = jnp.concatenate([
        cols(_C_CKV, 256), rope, _placed(cols(_C_SK, 256), SB_HEAD_DIM), _placed(cols(_C_FK, 256), SB_HEAD_DIM),
        cols(_C_GATE, 1024), cols(_C_MR, MERGE_RANK)], axis=1).astype(BF16)
    kvb = w_kvb.reshape(MLA_KV_RANK, HEADS, MLA_NOPE + HEAD_V)
    wkmla = jnp.pad(kvb[:, :, :MLA_NOPE], ((0, 0), (0, 0), (0, HEAD_PAD - MLA_NOPE))).reshape(MLA_KV_RANK, HEADS * HEAD_PAD)
    wvt = kvb[:, :, MLA_NOPE:].reshape(MLA_KV_RANK, HEADS * HEAD_V).T
    fbias = jnp.pad(fox_bias, (0, AUG - HEADS)).reshape(AUG, 1)
    return (wt, wr, q_norm.reshape(-1, 1), kv_norm.reshape(-1, 1), kv_norm.reshape(1, -1),
            w_qb.T.astype(BF16), wvt.astype(BF16), wkmla.astype(BF16), fbias)


def _proj(x, weights, consts, cos_t, sin_t, tab, kmem, vmemt):
    b, s, d = x.shape
    wt, wr, qn, kvnc, kvnr, wqbt, wvt, wkmla, fbias = weights
    selq, onesq, selk, onesk, p2 = consts
    full = lambda a: pl.BlockSpec(a.shape, lambda i, j: (0,) * a.ndim)
    qt_spec = pl.BlockSpec((1, HEADS * HEAD_PAD, TM), lambda i, j: (i, 0, j))
    k_spec = pl.BlockSpec((1, TM, HEADS * HEAD_PAD), lambda i, j: (i, j, 0))
    vt_spec = lambda tk: pl.BlockSpec((1, TM // tk, HEADS * HEAD_V, tk), lambda i, j: (i, j, 0, 0))
    qt_shape = jax.ShapeDtypeStruct((b, HEADS * HEAD_PAD, s), BF16)
    k_shape = jax.ShapeDtypeStruct((b, s, HEADS * HEAD_PAD), BF16)
    vt_shape = lambda tk: jax.ShapeDtypeStruct((b, s // tk, HEADS * HEAD_V, tk), BF16)
    return pl.pallas_call(
        _proj_body,
        grid=(b, s // TM),
        in_specs=[pl.BlockSpec((1, TM, d), lambda i, j: (i, j, 0)),
                  full(wt), full(wr), full(qn), full(kvnc), full(kvnr), full(wqbt), full(wvt), full(wkmla), full(p2),
                  pl.BlockSpec((1, MLA_ROPE // 2, TM), lambda i, j: (i, 0, j)),
                  pl.BlockSpec((1, MLA_ROPE // 2, TM), lambda i, j: (i, 0, j)),
                  pl.BlockSpec((1, TM, HEAD_PAD), lambda i, j: (i, j, 0)),
                  full(fbias), full(selq), full(onesq), full(selk), full(onesk),
                  pl.BlockSpec((1, HEADS, MEM_LEN, HEAD_V), lambda i, j: (i, 0, 0, 0)),
                  pl.BlockSpec((1, HEADS, HEAD_V, MEM_LEN), lambda i, j: (i, 0, 0, 0))],
        out_specs=[qt_spec, k_spec, vt_spec(TK_SM), qt_spec, k_spec, vt_spec(TK_SB), qt_spec, k_spec, vt_spec(TK_SM),
                   pl.BlockSpec((1, HEADS * HEAD_V, TM), lambda i, j: (i, 0, j)),
                   pl.BlockSpec((1, TM, N_BRANCHES * BRANCH_WIDTH), lambda i, j: (i, j, 0)),
                   pl.BlockSpec((1, TM, MERGE_RANK), lambda i, j: (i, j, 0))],
        out_shape=[qt_shape, k_shape, vt_shape(TK_SM), qt_shape, k_shape, vt_shape(TK_SB), qt_shape, k_shape,
                   vt_shape(TK_SM),
                   jax.ShapeDtypeStruct((b, HEADS * HEAD_V, s), F32),
                   jax.ShapeDtypeStruct((b, s, N_BRANCHES * BRANCH_WIDTH), BF16),
                   jax.ShapeDtypeStruct((b, s, MERGE_RANK), BF16)],
        scratch_shapes=[pltpu.VMEM((AUG, 128), F32)],
        compiler_params=pltpu.CompilerParams(dimension_semantics=("arbitrary", "arbitrary"),
                                             vmem_limit_bytes=VMEM_LIMIT),
        name="fused_projection",
    )(x, wt, wr, qn, kvnc, kvnr, wqbt, wvt, wkmla, p2, cos_t, sin_t, tab, fbias, selq, onesq, selk, onesk,
      kmem, vmemt)


def _causal_masks(tq, tk, strict):
    row = lax.broadcasted_iota(jnp.int32, (tk, tq), 0)
    col = lax.broadcasted_iota(jnp.int32, (tk, tq), 1)
    return [(row + j * tk < col) if strict else (row + j * tk <= col) for j in range(tq // tk)]


def _softmax_attn_body(q_ref, k_ref, v_ref, o_ref, acc_ref, s_ref, *, tq, tk, hb):
    nd = tq // tk
    qi = pl.program_id(2)
    n_full = qi * nd
    acc_ref[...] = jnp.zeros_like(acc_ref)

    def scores(k, h):
        rows = pl.ds(pl.multiple_of(k * tk, tk), tk)
        return _dot(k_ref[0, rows, h * HEAD_PAD:(h + 1) * HEAD_PAD], q_ref[0, h * HEAD_PAD:(h + 1) * HEAD_PAD, :])

    def step(k, slot, k_next, carry, mask):
        out = [None] * hb

        def prefetch(h):
            if k_next is not None and h < hb:
                s_ref[1 - slot, h] = scores(k_next, h)

        def softmax_pv(h):
            m, l = carry[h]
            s = s_ref[slot, h]
            if mask is not None:
                s = jnp.where(mask, s, NEG)
            m_new = jnp.maximum(m, jnp.max(s, axis=0, keepdims=True))
            alpha = jnp.exp2(m - m_new)
            p = jnp.exp2(s - m_new)
            if mask is not None:
                p = jnp.where(mask, p, 0.0)
            l = alpha * l + jnp.sum(p, axis=0, keepdims=True)
            hv = slice(h * HEAD_V, (h + 1) * HEAD_V)
            acc_ref[hv, :] = alpha * acc_ref[hv, :] + _dot(v_ref[0, k, hv, :], p.astype(BF16))
            out[h] = (m_new, l)

        prefetch(0)
        for h in range(hb):
            prefetch(h + 1)
            softmax_pv(h)
        return tuple(out)

    for h in range(hb):
        s_ref[0, h] = scores(0, h)

    def pair(i, carry):
        carry = step(2 * i, 0, 2 * i + 1, carry, None)
        return step(2 * i + 1, 1, 2 * i + 2, carry, None)

    carry = tuple((jnp.full((1, tq), NEG, F32), jnp.zeros((1, tq), F32)) for _ in range(hb))
    carry = lax.fori_loop(0, n_full // 2, pair, carry)
    masks = _causal_masks(tq, tk, strict=False)
    for j in range(nd):
        carry = step(n_full + j, j % 2, n_full + j + 1 if j + 1 < nd else None, carry, masks[j])
    for h in range(hb):
        hv = slice(h * HEAD_V, (h + 1) * HEAD_V)
        o_ref[0, hv, :] = acc_ref[hv, :] * (1.0 / carry[h][1])


def _sb_attn_body(q_ref, k_ref, v_ref, ntri_ref, o_ref, acc_ref, s_ref, *, tq, tk, hb):
    nd = tq // tk
    qi = pl.program_id(2)
    n_full = qi * nd
    acc_ref[...] = jnp.zeros_like(acc_ref)

    def scores(k, h):
        rows = pl.ds(pl.multiple_of(k * tk, tk), tk)
        return _dot(k_ref[0, rows, h * HEAD_PAD:(h + 1) * HEAD_PAD], q_ref[0, h * HEAD_PAD:(h + 1) * HEAD_PAD, :])

    def step(k, slot, carry, mask):
        k_next = jnp.maximum(k - 1, 0)
        out = [None] * hb
        local = [None] * hb

        def suffix(h):
            if h >= hb:
                return
            z = s_ref[slot, h]
            neg_abs = pltpu.bitcast(pltpu.bitcast(z, jnp.uint32) | jnp.uint32(0x80000000), F32)
            sp = jnp.maximum(z, 0.0) + jnp.log(1.0 + jnp.exp2(neg_abs)) * LOG2E
            if mask is not None:
                sp = jnp.where(mask, sp, 0.0)
            local[h] = _dot(ntri_ref[...], sp.astype(BF16))
            s_ref[1 - slot, h] = scores(k_next, h)

        def weights_pv(h):
            cross = carry[h]
            a = jnp.exp2(s_ref[slot, h] + local[h])
            if mask is not None:
                a = jnp.where(mask, a, 0.0)
            hv = slice(h * HEAD_V, (h + 1) * HEAD_V)
            acc_ref[hv, :] += _dot(v_ref[0, k, hv, :], a.astype(BF16)) * jnp.exp2(cross)
            out[h] = cross + local[h][0:1, :]

        suffix(0)
        for h in range(hb):
            suffix(h + 1)
            weights_pv(h)
        return tuple(out)

    for h in range(hb):
        s_ref[(nd - 1) % 2, h] = scores(n_full + nd - 1, h)
    carry = tuple(jnp.zeros((1, tq), F32) for _ in range(hb))
    masks = _causal_masks(tq, tk, strict=True)
    for j in reversed(range(nd)):
        carry = step(n_full + j, j % 2, carry, masks[j])

    def pair(i, carry):
        k = n_full - 1 - 2 * i
        carry = step(k, 1, carry, None)
        return step(k - 1, 0, carry, None)

    def alive(carry):
        top = functools.reduce(jnp.maximum, carry)
        return jnp.max(top) > SB_DEAD_LOG2

    def cond(state):
        i, live, _ = state
        return jnp.logical_and(i < n_full // 2, live)

    def body(state):
        i, _, carry = state
        carry = pair(i, carry)
        return i + 1, alive(carry), carry

    lax.while_loop(cond, body, (jnp.int32(0), alive(carry), carry))
    o_ref[0] = acc_ref[...]


def _attention(name, body, q_t, k, v_t, tk, hb, extra=()):
    b, _, s = q_t.shape
    in_specs = [pl.BlockSpec((1, hb * HEAD_PAD, TQ), lambda i, g, j: (i, g, j)),
                pl.BlockSpec((1, s, hb * HEAD_PAD), lambda i, g, j: (i, 0, g)),
                pl.BlockSpec((1, s // tk, hb * HEAD_V, tk), lambda i, g, j: (i, 0, g, 0))]
    in_specs += [pl.BlockSpec(e.shape, lambda i, g, j: (0,) * e.ndim) for e in extra]
    return pl.pallas_call(
        functools.partial(body, tq=TQ, tk=tk, hb=hb),
        grid=(b, HEADS // hb, s // TQ),
        in_specs=in_specs,
        out_specs=pl.BlockSpec((1, hb * HEAD_V, TQ), lambda i, g, j: (i, g, j)),
        out_shape=jax.ShapeDtypeStruct((b, HEADS * HEAD_V, s), F32),
        scratch_shapes=[pltpu.VMEM((hb * HEAD_V, TQ), F32), pltpu.VMEM((2, hb, tk, TQ), F32)],
        compiler_params=pltpu.CompilerParams(dimension_semantics=("arbitrary",) * 3,
                                             vmem_limit_bytes=VMEM_LIMIT),
        name=name,
    )(q_t, k, v_t, *extra)


def _out_body(x_ref, ymla_ref, ysb_ref, yfox_ref, ymem_ref, gate_ref, mr_ref, wmu_ref, wb_ref, wo_ref,
              g_ref, b_ref, o_ref):
    tm = x_ref.shape[1]
    parts = [slice(i * (tm // OUT_SPLIT), (i + 1) * (tm // OUT_SPLIT)) for i in range(OUT_SPLIT)]

    def merge(rows):
        mr = mr_ref[0, rows, :]
        merged = None
        for n, y_ref in enumerate((ymla_ref, ysb_ref, yfox_ref, ymem_ref)):
            y = y_ref[0, :, rows].T
            gz = gate_ref[0, rows, n * BRANCH_WIDTH:(n + 1) * BRANCH_WIDTH].astype(F32)
            yb = (y * (gz * _sigmoid(gz))).astype(BF16)
            branch = _dot(yb, wb_ref[n])
            mg = _sigmoid(_dot(mr, wmu_ref[:, n * D_MODEL:(n + 1) * D_MODEL]))
            merged = mg * branch if merged is None else merged + mg * branch
        return merged.astype(BF16)

    merged = [merge(rows) for rows in parts]
    outs = [_dot(m, wo_ref[...]) for m in merged]
    for rows, out in zip(parts, outs):
        r = DEEPNORM_ALPHA * x_ref[0, rows, :] + out
        mu = jnp.mean(r, axis=-1, keepdims=True)
        var = jnp.mean(jnp.square(r - mu), axis=-1, keepdims=True)
        o_ref[0, rows, :] = (r - mu) * lax.rsqrt(var + LN_EPS) * g_ref[...] + b_ref[...]


def _out(x, ys, gate, mr, wmu, wb, wo, g, bias):
    b, s, d = x.shape
    full = lambda a: pl.BlockSpec(a.shape, lambda i, j: (0,) * a.ndim)
    yt_spec = pl.BlockSpec((1, BRANCH_WIDTH, TM), lambda i, j: (i, 0, j))
    return pl.pallas_call(
        _out_body,
        grid=(b, s // TM),
        in_specs=[pl.BlockSpec((1, TM, d), lambda i, j: (i, j, 0)), yt_spec, yt_spec, yt_spec, yt_spec,
                  pl.BlockSpec((1, TM, N_BRANCHES * BRANCH_WIDTH), lambda i, j: (i, j, 0)),
                  pl.BlockSpec((1, TM, MERGE_RANK), lambda i, j: (i, j, 0)),
                  full(wmu), full(wb), full(wo), full(g), full(bias)],
        out_specs=pl.BlockSpec((1, TM, d), lambda i, j: (i, j, 0)),
        out_shape=jax.ShapeDtypeStruct((b, s, d), F32),
        compiler_params=pltpu.CompilerParams(dimension_semantics=("arbitrary", "arbitrary"),
                                             vmem_limit_bytes=VMEM_LIMIT),
        name="fused_output",
    )(x, *ys, gate, mr, wmu, wb, wo, g, bias)


def kernel(x, mem, positions, w_in, mla_q_norm, mla_w_qb, mla_kv_norm, mla_w_kvb, fox_forget_bias, w_mem_kv,
           w_merge_up, w_branch, w_out, ln_gain, ln_bias):
    b, s, d = x.shape
    assert d == D_MODEL and s % TM == 0 and s % TQ == 0
    assert all(TQ % (2 * tk) == 0 and TM % tk == 0 for tk in (TK_SM, TK_SB)) and HEADS % HB == 0
    depth = w_in.shape[0]

    cos_t, sin_t = _rope_tables(positions)
    cos_r, sin_r = cos_t.transpose(0, 2, 1), sin_t.transpose(0, 2, 1)
    tab = jnp.concatenate([cos_r, cos_r, sin_r, sin_r, jnp.zeros((b, s, HEAD_PAD - 2 * MLA_ROPE), F32)], axis=-1)

    mkv = _memkv(mem, w_mem_kv).reshape(depth, b, MEM_LEN, 2, HEADS, HEAD_V)
    kmem = mkv[:, :, :, 0].transpose(0, 1, 3, 2, 4).astype(BF16)
    vmemt = mkv[:, :, :, 1].transpose(0, 1, 3, 4, 2).astype(BF16)

    consts = _selection_constants()
    ntri = jnp.asarray(-np.triu(np.ones((TK_SB, TK_SB), np.float32)), BF16)

    for l in range(depth):
        weights = _proj_weights(w_in[l], mla_q_norm[l], mla_w_qb[l], mla_kv_norm[l], mla_w_kvb[l],
                                fox_forget_bias[l])
        (qmla, kmla, vmla, qsb, ksb, vsb, qfox, kfox, vfox, ymem, gate, mr) = _proj(
            x, weights, consts, cos_t, sin_t, tab, kmem[l], vmemt[l])
        ymla = _attention("mla_attention", _softmax_attn_body, qmla, kmla, vmla, TK_SM, HB)
        ysb = _attention("stick_breaking_attention", _sb_attn_body, qsb, ksb, vsb, TK_SB, HB, extra=(ntri,))
        yfox = _attention("forgetting_attention", _softmax_attn_body, qfox, kfox, vfox, TK_SM, HB)
        x = _out(x, (ymla, ysb, yfox, ymem), gate, mr, w_merge_up[l].astype(BF16), w_branch[l].astype(BF16),
                 w_out[l].astype(BF16), ln_gain[l].reshape(1, d), ln_bias[l].reshape(1, d))
    return x
```

```python
import functools

import jax
import jax.numpy as jnp
import numpy as np
from jax import lax
from jax.experimental import pallas as pl
from jax.experimental.pallas import tpu as pltpu

F32, BF16 = jnp.float32, jnp.bfloat16

D_MODEL = 1024
DEPTH = 4
MEM_LEN = 256
N_BRANCHES = 4
BRANCH_WIDTH = 256
HEADS = 4
HEAD_V = 64
MLA_NOPE = 64
MLA_ROPE = 32
MLA_Q_RANK = 384
MLA_KV_RANK = 256
ROPE_THETA = 10000.0
SB_HEAD_DIM = 64
MERGE_RANK = 128
RMS_EPS = 1e-6
LN_EPS = 1e-5
DEEPNORM_ALPHA = (2 * DEPTH) ** 0.25

_C_CQ, _C_CKV, _C_KR = 0, 384, 640
_C_SQ, _C_SK, _C_SV = 672, 928, 1184
_C_FQ, _C_FK, _C_FV = 1440, 1696, 1952
_C_FF, _C_MQ, _C_GATE, _C_MR = 2208, 2212, 2468, 3492

HEAD_PAD = 128
TM = 512
TQ = 512
TK_SM = 256
TK_SB = 256
HB = 4
OUT_SPLIT = 4
LOG2E = 1.4426950408889634
DEAD_LOG2 = -160.0
NEG = -1e30
VMEM_LIMIT = 56 * 1024 * 1024

_T_CQ, _T_CKV, _T_SQ, _T_SV, _T_FQ, _T_FV, _T_MQ, _T_FF, _T_END = 0, 384, 640, 896, 1152, 1408, 1664, 1920, 1936
_R_CKV, _R_ROPE, _R_SK, _R_FK, _R_GATE, _R_MR, _R_END = 0, 256, 384, 896, 1408, 2432, 2560
AUG = 16


def _dot(a, b):
    return jnp.dot(a, b, preferred_element_type=F32)


def _dot_nt(a, b):
    return lax.dot_general(a, b, (((1,), (1,)), ((), ())), preferred_element_type=F32)


def _sigmoid(v):
    return 0.5 * jnp.tanh(0.5 * v) + 0.5


def _split3(v):
    p1 = v.astype(BF16).astype(F32)
    r1 = v - p1
    p2 = r1.astype(BF16).astype(F32)
    p3 = (r1 - p2).astype(BF16).astype(F32)
    return p1, p2, p3


def _rope_body(pos_ref, invf_ref, cos_ref, sin_ref):
    ang = pos_ref[0].astype(F32) * invf_ref[...]
    cos_ref[0] = jnp.cos(ang)
    sin_ref[0] = jnp.sin(ang)


def _rope_tables(positions):
    b, s = positions.shape
    half = MLA_ROPE // 2
    inv_freq = ROPE_THETA ** (-jnp.arange(0, MLA_ROPE, 2, dtype=F32) / MLA_ROPE)
    return pl.pallas_call(
        _rope_body,
        grid=(b,),
        in_specs=[pl.BlockSpec((1, 1, s), lambda i: (i, 0, 0)),
                  pl.BlockSpec((half, 1), lambda i: (0, 0))],
        out_specs=[pl.BlockSpec((1, half, s), lambda i: (i, 0, 0))] * 2,
        out_shape=[jax.ShapeDtypeStruct((b, half, s), F32)] * 2,
        name="rope_tables",
    )(positions.reshape(b, 1, s), inv_freq.reshape(half, 1))


def _memkv_body(mem_ref, w_ref, o_ref):
    o_ref[0] = _dot(mem_ref[...].astype(BF16), w_ref[0].astype(BF16))


def _memkv(mem, w_mem_kv):
    b, m, d = mem.shape
    depth, _, n = w_mem_kv.shape
    return pl.pallas_call(
        _memkv_body,
        grid=(depth,),
        in_specs=[pl.BlockSpec((b * m, d), lambda l: (0, 0)),
                  pl.BlockSpec((1, d, n), lambda l: (l, 0, 0))],
        out_specs=pl.BlockSpec((1, b * m, n), lambda l: (l, 0, 0)),
        out_shape=jax.ShapeDtypeStruct((depth, b * m, n), F32),
        compiler_params=pltpu.CompilerParams(vmem_limit_bytes=VMEM_LIMIT),
        name="mem_kv_proj",
    )(mem.reshape(b * m, d), w_mem_kv)


def _proj_body(x_ref, wt_ref, wr_ref, qn_ref, kvnc_ref, kvnr_ref, wqbt_ref, wvt_ref, wkmla_ref, p2_ref,
               cos_ref, sin_ref, fbias_ref, selq_ref, onesq_ref, selk_ref, onesk_ref, knsel_ref,
               kmem_ref, vmemt_ref,
               qmla_o, kmla_o, vmla_o, qsb_o, ksb_o, vsb_o, qfox_o, kfox_o, vfox_o, ymem_o, gate_o, mr_o,
               kn_o, cend_o, carry_ref):
    tm = x_ref.shape[1]
    xb = x_ref[0].astype(BF16)

    def store_vt(o_ref, vt):
        tk = o_ref.shape[3]
        for j in range(tm // tk):
            o_ref[0, j] = vt[:, j * tk:(j + 1) * tk].astype(BF16)

    def rms(v, axis, g):
        return (v * lax.rsqrt(jnp.mean(v * v, axis=axis, keepdims=True) + RMS_EPS) * g).astype(BF16)

    ff = _dot_nt(wt_ref[_T_FF:_T_END, :], xb) + fbias_ref[...]
    lf = jnp.minimum(ff, 0.0) - jnp.log1p(jnp.exp(-jnp.abs(ff)))
    upper = (lax.broadcasted_iota(jnp.int32, (tm, tm), 0) <= lax.broadcasted_iota(jnp.int32, (tm, tm), 1))
    cqn = rms(_dot_nt(wt_ref[_T_CQ:_T_CKV, :], xb), 0, qn_ref[...])
    ckvn_t = rms(_dot_nt(wt_ref[_T_CKV:_T_SQ, :], xb), 0, kvnc_ref[...])
    ckvn_r = rms(_dot(xb, wr_ref[:, _R_CKV:_R_ROPE]), 1, kvnr_ref[...])
    cos_t, sin_t = cos_ref[0], sin_ref[0]
    tab = jnp.concatenate([cos_t, cos_t, sin_t, sin_t, jnp.zeros((HEAD_PAD - 2 * MLA_ROPE, tm), F32)], axis=0).T
    prod = _dot(xb, wr_ref[:, _R_ROPE:_R_SK]) * tab
    kpe = (prod + pltpu.roll(prod, 128 - MLA_ROPE, 1)).astype(BF16)
    mq = (_dot_nt(wt_ref[_T_MQ:_T_FF, :], xb) * (HEAD_V ** -0.5)).astype(BF16)

    csum = _dot(jnp.concatenate(_split3(lf), axis=0).astype(BF16), upper.astype(BF16))

    @pl.when(pl.program_id(1) == 0)
    def _():
        carry_ref[...] = jnp.zeros_like(carry_ref)

    c = csum[0:AUG] + csum[AUG:2 * AUG] + csum[2 * AUG:3 * AUG] + carry_ref[:, 0:1]
    carry_ref[...] = jnp.broadcast_to(c[:, tm - 1:tm], carry_ref.shape)
    c_log2 = c * LOG2E
    cst = jnp.concatenate([*_split3(c_log2), jnp.zeros((HEAD_PAD - 3 * AUG, tm), F32)], axis=0)
    tk_fox = vfox_o.shape[3]
    for j in range(tm // tk_fox):
        cend_o[0, j] = jnp.broadcast_to(c_log2[:, (j + 1) * tk_fox - 1:(j + 1) * tk_fox], cend_o.shape[2:])

    sq = _dot_nt(wt_ref[_T_SQ:_T_SV, :], xb) * (SB_HEAD_DIM ** -0.5 * LOG2E)
    zs = jnp.zeros((HEAD_PAD - SB_HEAD_DIM, tm), F32)
    for h in range(HEADS):
        blk = jnp.concatenate([sq[h * SB_HEAD_DIM:(h + 1) * SB_HEAD_DIM], zs], axis=0)
        qsb_o[0, h * HEAD_PAD:(h + 1) * HEAD_PAD, :] = blk.astype(BF16)
    q = _dot(wqbt_ref[...], cqn) * ((MLA_NOPE + MLA_ROPE) ** -0.5 * LOG2E)
    half = MLA_ROPE // 2
    dq = MLA_NOPE + MLA_ROPE
    zq = jnp.zeros((HEAD_PAD - dq, tm), F32)
    for h in range(HEADS):
        nope = q[h * dq:h * dq + MLA_NOPE]
        x1 = q[h * dq + MLA_NOPE:h * dq + MLA_NOPE + half]
        x2 = q[h * dq + MLA_NOPE + half:(h + 1) * dq]
        blk = jnp.concatenate([nope, x1 * cos_t - x2 * sin_t, x1 * sin_t + x2 * cos_t, zq], axis=0)
        qmla_o[0, h * HEAD_PAD:(h + 1) * HEAD_PAD, :] = blk.astype(BF16)

    store_vt(vsb_o, _dot_nt(wt_ref[_T_SV:_T_FQ, :], xb))
    store_vt(vmla_o, _dot(wvt_ref[...], ckvn_t))
    ksb_o[0] = _dot(xb, wr_ref[:, _R_SK:_R_FK]).astype(BF16)
    kmla_o[0] = (_dot(ckvn_r, wkmla_ref[...]) + _dot(kpe, p2_ref[...])).astype(BF16)

    fq = _dot_nt(wt_ref[_T_FQ:_T_FV, :], xb) * (SB_HEAD_DIM ** -0.5 * LOG2E)
    aug_t = _dot(selq_ref[...], cst.astype(BF16)) + onesq_ref[...]
    zf = jnp.zeros((HEAD_PAD - SB_HEAD_DIM - 8, tm), F32)
    for h in range(HEADS):
        blk = jnp.concatenate([fq[h * SB_HEAD_DIM:(h + 1) * SB_HEAD_DIM], aug_t[h * 8:(h + 1) * 8], zf], axis=0)
        qfox_o[0, h * HEAD_PAD:(h + 1) * HEAD_PAD, :] = blk.astype(BF16)
    store_vt(vfox_o, _dot_nt(wt_ref[_T_FV:_T_MQ, :], xb))
    kfox = _dot(xb, wr_ref[:, _R_FK:_R_GATE]) + _dot(cst.T.astype(BF16), selk_ref[...]) + onesk_ref[...]
    kfox = kfox.astype(BF16)
    kfox_o[0] = kfox
    ksq = jnp.square(kfox.astype(F32)).astype(BF16)
    kn_o[0, 0] = jnp.broadcast_to(jnp.max(_dot(ksq, knsel_ref[...]), axis=0, keepdims=True), kn_o.shape[2:])

    s_mem = [_dot(kmem_ref[0, h], mq[h * HEAD_V:(h + 1) * HEAD_V]) for h in range(HEADS)]
    gate_o[0] = _dot(xb, wr_ref[:, _R_GATE:_R_MR]).astype(BF16)
    mr_o[0] = _dot(xb, wr_ref[:, _R_MR:_R_END]).astype(BF16)
    for h in range(HEADS):
        p = jnp.exp(s_mem[h] - jnp.max(s_mem[h], axis=0, keepdims=True))
        o = _dot(vmemt_ref[0, h], p.astype(BF16))
        ymem_o[0, h * HEAD_V:(h + 1) * HEAD_V, :] = o / jnp.sum(p, axis=0, keepdims=True)


def _placed(w, n_in):
    depth, d, _ = w.shape
    w = w.reshape(depth, d, HEADS, n_in)
    return jnp.pad(w, ((0, 0), (0, 0), (0, 0), (0, HEAD_PAD - n_in))).reshape(depth, d, HEADS * HEAD_PAD)


def _selection_constants():
    selq = np.zeros((HEADS * 8, HEAD_PAD), np.float32)
    onesq = np.zeros((HEADS * 8, 1), np.float32)
    selk = np.zeros((HEAD_PAD, HEADS * HEAD_PAD), np.float32)
    onesk = np.zeros((1, HEADS * HEAD_PAD), np.float32)
    p2 = np.zeros((HEAD_PAD, HEADS * HEAD_PAD), np.float32)
    for h in range(HEADS):
        for i in range(3):
            selq[h * 8 + i, i * AUG + h] = 1.0
            onesq[h * 8 + 3 + i, 0] = 1.0
            onesk[0, h * HEAD_PAD + SB_HEAD_DIM + i] = 1.0
            selk[i * AUG + h, h * HEAD_PAD + SB_HEAD_DIM + 3 + i] = -1.0
        for j in range(MLA_ROPE):
            p2[j, h * HEAD_PAD + MLA_NOPE + j] = 1.0
    knsel = np.zeros((HEADS * HEAD_PAD, 128), np.float32)
    for h in range(HEADS):
        knsel[h * HEAD_PAD:h * HEAD_PAD + SB_HEAD_DIM, 32 * h:32 * (h + 1)] = 1.0
    return (jnp.asarray(selq, BF16), jnp.asarray(onesq), jnp.asarray(selk, BF16), jnp.asarray(onesk),
            jnp.asarray(p2, BF16), jnp.asarray(knsel, BF16))


def _proj_weights(w_in, q_norm, w_qb, kv_norm, w_kvb, fox_bias):
    depth = w_in.shape[0]
    cols = lambda a, n: w_in[:, :, a:a + n]
    t = lambda a: jnp.swapaxes(a, 1, 2)
    wt = jnp.concatenate([
        t(cols(_C_CQ, 384)), t(cols(_C_CKV, 256)), t(cols(_C_SQ, 256)), t(cols(_C_SV, 256)),
        t(cols(_C_FQ, 256)), t(cols(_C_FV, 256)), t(cols(_C_MQ, 256)), t(cols(_C_FF, 4)),
        jnp.zeros((depth, _T_END - _T_FF - HEADS, D_MODEL), F32)], axis=1).astype(BF16)
    kr = cols(_C_KR, MLA_ROPE)
    half = MLA_ROPE // 2
    rope = jnp.concatenate([kr, -kr[:, :, half:], kr[:, :, :half],
                            jnp.zeros((depth, D_MODEL, HEAD_PAD - 2 * MLA_ROPE), F32)], axis=2)
    wr = jnp.concatenate([
        cols(_C_CKV, 256), rope, _placed(cols(_C_SK, 256), SB_HEAD_DIM), _placed(cols(_C_FK, 256), SB_HEAD_DIM),
        cols(_C_GATE, 1024), cols(_C_MR, MERGE_RANK)], axis=2).astype(BF16)
    kvb = w_kvb.reshape(depth, MLA_KV_RANK, HEADS, MLA_NOPE + HEAD_V)
    wkmla = jnp.pad(kvb[..., :MLA_NOPE], ((0, 0), (0, 0), (0, 0), (0, HEAD_PAD - MLA_NOPE)))
    wkmla = wkmla.reshape(depth, MLA_KV_RANK, HEADS * HEAD_PAD)
    wvt = t(kvb[..., MLA_NOPE:].reshape(depth, MLA_KV_RANK, HEADS * HEAD_V))
    fbias = jnp.pad(fox_bias, ((0, 0), (0, AUG - HEADS)))[:, :, None]
    return (wt, wr, q_norm[:, :, None], kv_norm[:, :, None], kv_norm[:, None, :],
            t(w_qb).astype(BF16), wvt.astype(BF16), wkmla.astype(BF16), fbias)


def _layer_spec(a, l):
    return pl.BlockSpec((None,) + a.shape[1:], lambda i, j: (l,) + (0,) * (a.ndim - 1))


def _proj(x, l, weights, consts, cos_t, sin_t, kmem, vmemt):
    b, s, d = x.shape
    lay = lambda a: _layer_spec(a, l)
    wt, wr, qn, kvnc, kvnr, wqbt, wvt, wkmla, fbias = weights
    selq, onesq, selk, onesk, p2, knsel = consts
    full = lambda a: pl.BlockSpec(a.shape, lambda i, j: (0,) * a.ndim)
    qt_spec = pl.BlockSpec((1, HEADS * HEAD_PAD, TM), lambda i, j: (i, 0, j))
    k_spec = pl.BlockSpec((1, TM, HEADS * HEAD_PAD), lambda i, j: (i, j, 0))
    vt_spec = lambda tk: pl.BlockSpec((1, TM // tk, HEADS * HEAD_V, tk), lambda i, j: (i, j, 0, 0))
    qt_shape = jax.ShapeDtypeStruct((b, HEADS * HEAD_PAD, s), BF16)
    k_shape = jax.ShapeDtypeStruct((b, s, HEADS * HEAD_PAD), BF16)
    vt_shape = lambda tk: jax.ShapeDtypeStruct((b, s // tk, HEADS * HEAD_V, tk), BF16)
    return pl.pallas_call(
        _proj_body,
        grid=(b, s // TM),
        in_specs=[pl.BlockSpec((1, TM, d), lambda i, j: (i, j, 0)),
                  lay(wt), lay(wr), lay(qn), lay(kvnc), lay(kvnr), lay(wqbt), lay(wvt), lay(wkmla), full(p2),
                  pl.BlockSpec((1, MLA_ROPE // 2, TM), lambda i, j: (i, 0, j)),
                  pl.BlockSpec((1, MLA_ROPE // 2, TM), lambda i, j: (i, 0, j)),
                  lay(fbias), full(selq), full(onesq), full(selk), full(onesk), full(knsel),
                  pl.BlockSpec((None, 1, HEADS, MEM_LEN, HEAD_V), lambda i, j: (l, i, 0, 0, 0)),
                  pl.BlockSpec((None, 1, HEADS, HEAD_V, MEM_LEN), lambda i, j: (l, i, 0, 0, 0))],
        out_specs=[qt_spec, k_spec, vt_spec(TK_SM), qt_spec, k_spec, vt_spec(TK_SB), qt_spec, k_spec, vt_spec(TK_SM),
                   pl.BlockSpec((1, HEADS * HEAD_V, TM), lambda i, j: (i, 0, j)),
                   pl.BlockSpec((1, TM, N_BRANCHES * BRANCH_WIDTH), lambda i, j: (i, j, 0)),
                   pl.BlockSpec((1, TM, MERGE_RANK), lambda i, j: (i, j, 0)),
                   pl.BlockSpec((1, 1, 8, 128), lambda i, j: (i, j, 0, 0)),
                   pl.BlockSpec((1, TM // TK_SM, AUG, 128), lambda i, j: (i, j, 0, 0))],
        out_shape=[qt_shape, k_shape, vt_shape(TK_SM), qt_shape, k_shape, vt_shape(TK_SB), qt_shape, k_shape,
                   vt_shape(TK_SM),
                   jax.ShapeDtypeStruct((b, HEADS * HEAD_V, s), F32),
                   jax.ShapeDtypeStruct((b, s, N_BRANCHES * BRANCH_WIDTH), BF16),
                   jax.ShapeDtypeStruct((b, s, MERGE_RANK), BF16),
                   jax.ShapeDtypeStruct((b, s // TM, 8, 128), F32),
                   jax.ShapeDtypeStruct((b, s // TK_SM, AUG, 128), F32)],
        scratch_shapes=[pltpu.VMEM((AUG, 128), F32)],
        compiler_params=pltpu.CompilerParams(dimension_semantics=("arbitrary", "arbitrary"),
                                             vmem_limit_bytes=VMEM_LIMIT),
        name="fused_projection",
    )(x, wt, wr, qn, kvnc, kvnr, wqbt, wvt, wkmla, p2, cos_t, sin_t, fbias, selq, onesq, selk, onesk,
      knsel, kmem, vmemt)


def _causal_masks(tq, tk, strict):
    row = lax.broadcasted_iota(jnp.int32, (tk, tq), 0)
    col = lax.broadcasted_iota(jnp.int32, (tk, tq), 1)
    return [(row + j * tk < col) if strict else (row + j * tk <= col) for j in range(tq // tk)]


def _softmax_attn_body(q_ref, k_ref, v_ref, *rest, tq, tk, hb, decay):
    if decay:
        kn_ref, cend_ref, o_ref, acc_ref, s_ref = rest
    else:
        o_ref, acc_ref, s_ref = rest
    nd = tq // tk
    qi = pl.program_id(2)
    n_full = qi * nd
    acc_ref[...] = jnp.zeros_like(acc_ref)

    def scores(k, h):
        rows = pl.ds(pl.multiple_of(k * tk, tk), tk)
        return _dot(k_ref[0, rows, h * HEAD_PAD:(h + 1) * HEAD_PAD], q_ref[0, h * HEAD_PAD:(h + 1) * HEAD_PAD, :])

    def step(k, slot, carry, mask):
        k_next = jnp.maximum(k - 1, 0)
        out = [None] * hb

        def prefetch(h):
            if h < hb:
                s_ref[1 - slot, h] = scores(k_next, h)

        def softmax_pv(h):
            m, l = carry[h]
            s = s_ref[slot, h]
            if mask is not None:
                s = jnp.where(mask, s, NEG)
            m_new = jnp.maximum(m, jnp.max(s, axis=0, keepdims=True))
            alpha = jnp.exp2(m - m_new)
            p = jnp.exp2(s - m_new)
            if mask is not None:
                p = jnp.where(mask, p, 0.0)
            l = alpha * l + jnp.sum(p, axis=0, keepdims=True)
            hv = slice(h * HEAD_V, (h + 1) * HEAD_V)
            acc_ref[hv, :] = alpha * acc_ref[hv, :] + _dot(v_ref[0, k, hv, :], p.astype(BF16))
            out[h] = (m_new, l)

        prefetch(0)
        for h in range(hb):
            prefetch(h + 1)
            softmax_pv(h)
        return tuple(out)

    for h in range(hb):
        s_ref[(nd - 1) % 2, h] = scores(n_full + nd - 1, h)
    carry = tuple((jnp.full((1, tq), NEG, F32), jnp.zeros((1, tq), F32)) for _ in range(hb))
    masks = _causal_masks(tq, tk, strict=False)
    for j in reversed(range(nd)):
        carry = step(n_full + j, j % 2, carry, masks[j])

    def pair(i, carry):
        k = n_full - 1 - 2 * i
        carry = step(k, 1, carry, None)
        return step(k - 1, 0, carry, None)

    if not decay:
        carry = lax.fori_loop(0, n_full // 2, pair, carry)
    else:
        assert hb == HEADS
        kmax = jnp.sqrt(jnp.max(kn_ref[0], axis=0)[0:1, :]) * 1.01
        reach = []
        for h in range(hb):
            qh = q_ref[0, h * HEAD_PAD:h * HEAD_PAD + SB_HEAD_DIM + 8, :].astype(F32)
            qnorm = jnp.sqrt(jnp.sum(jnp.square(qh[0:SB_HEAD_DIM]), axis=0, keepdims=True))
            c_t = jnp.sum(qh[SB_HEAD_DIM:SB_HEAD_DIM + 3], axis=0, keepdims=True)
            reach.append(qnorm * kmax[:, 32 * h:32 * h + 1] + c_t)

        def alive(carry, k_low):
            c_end = cend_ref[0, jnp.maximum(k_low - 1, 0)]
            top = None
            for h in range(hb):
                gap = reach[h] - c_end[h:h + 1, 0:1] - carry[h][0]
                top = gap if top is None else jnp.maximum(top, gap)
            return jnp.max(top) > DEAD_LOG2

        def cond(state):
            i, live, _ = state
            return jnp.logical_and(i < n_full // 2, live)

        def body(state):
            i, _, carry = state
            carry = pair(i, carry)
            return i + 1, alive(carry, n_full - 2 * (i + 1)), carry

        carry = lax.while_loop(cond, body, (jnp.int32(0), alive(carry, n_full), carry))[2]
    for h in range(hb):
        hv = slice(h * HEAD_V, (h + 1) * HEAD_V)
        o_ref[0, hv, :] = acc_ref[hv, :] * (1.0 / carry[h][1])


def _sb_attn_body(q_ref, k_ref, v_ref, ntri_ref, o_ref, acc_ref, s_ref, *, tq, tk, hb):
    nd = tq // tk
    qi = pl.program_id(2)
    n_full = qi * nd
    acc_ref[...] = jnp.zeros_like(acc_ref)

    def scores(k, h):
        rows = pl.ds(pl.multiple_of(k * tk, tk), tk)
        return _dot(k_ref[0, rows, h * HEAD_PAD:(h + 1) * HEAD_PAD], q_ref[0, h * HEAD_PAD:(h + 1) * HEAD_PAD, :])

    def step(k, slot, carry, mask):
        k_next = jnp.maximum(k - 1, 0)
        out = [None] * hb
        local = [None] * hb

        def suffix(h):
            if h >= hb:
                return
            z = s_ref[slot, h]
            neg_abs = pltpu.bitcast(pltpu.bitcast(z, jnp.uint32) | jnp.uint32(0x80000000), F32)
            sp = jnp.maximum(z, 0.0) + jnp.log(1.0 + jnp.exp2(neg_abs)) * LOG2E
            if mask is not None:
                sp = jnp.where(mask, sp, 0.0)
            local[h] = _dot(ntri_ref[...], sp.astype(BF16))
            s_ref[1 - slot, h] = scores(k_next, h)

        def weights_pv(h):
            cross = carry[h]
            a = jnp.exp2(s_ref[slot, h] + local[h])
            if mask is not None:
                a = jnp.where(mask, a, 0.0)
            hv = slice(h * HEAD_V, (h + 1) * HEAD_V)
            acc_ref[hv, :] += _dot(v_ref[0, k, hv, :], a.astype(BF16)) * jnp.exp2(cross)
            out[h] = cross + local[h][0:1, :]

        suffix(0)
        for h in range(hb):
            suffix(h + 1)
            weights_pv(h)
        return tuple(out)

    for h in range(hb):
        s_ref[(nd - 1) % 2, h] = scores(n_full + nd - 1, h)
    carry = tuple(jnp.zeros((1, tq), F32) for _ in range(hb))
    masks = _causal_masks(tq, tk, strict=True)
    for j in reversed(range(nd)):
        carry = step(n_full + j, j % 2, carry, masks[j])

    def pair(i, carry):
        k = n_full - 1 - 2 * i
        carry = step(k, 1, carry, None)
        return step(k - 1, 0, carry, None)

    def alive(carry):
        top = functools.reduce(jnp.maximum, carry)
        return jnp.max(top) > DEAD_LOG2

    def cond(state):
        i, live, _ = state
        return jnp.logical_and(i < n_full // 2, live)

    def body(state):
        i, _, carry = state
        carry = pair(i, carry)
        return i + 1, alive(carry), carry

    lax.while_loop(cond, body, (jnp.int32(0), alive(carry), carry))
    o_ref[0] = acc_ref[...]


def _attention(name, body, q_t, k, v_t, tk, hb, extra=()):
    b, _, s = q_t.shape
    in_specs = [pl.BlockSpec((1, hb * HEAD_PAD, TQ), lambda i, g, j: (i, g, j)),
                pl.BlockSpec((1, s, hb * HEAD_PAD), lambda i, g, j: (i, 0, g)),
                pl.BlockSpec((1, s // tk, hb * HEAD_V, tk), lambda i, g, j: (i, 0, g, 0))]
    in_specs += [spec for _, spec in extra]
    return pl.pallas_call(
        functools.partial(body, tq=TQ, tk=tk, hb=hb),
        grid=(b, HEADS // hb, s // TQ),
        in_specs=in_specs,
        out_specs=pl.BlockSpec((1, hb * HEAD_V, TQ), lambda i, g, j: (i, g, j)),
        out_shape=jax.ShapeDtypeStruct((b, HEADS * HEAD_V, s), F32),
        scratch_shapes=[pltpu.VMEM((hb * HEAD_V, TQ), F32), pltpu.VMEM((2, hb, tk, TQ), F32)],
        compiler_params=pltpu.CompilerParams(dimension_semantics=("arbitrary",) * 3,
                                             vmem_limit_bytes=VMEM_LIMIT),
        name=name,
    )(q_t, k, v_t, *[a for a, _ in extra])


def _out_body(x_ref, ymla_ref, ysb_ref, yfox_ref, ymem_ref, gate_ref, mr_ref, wmu_ref, wb_ref, wo_ref,
              g_ref, b_ref, o_ref):
    tm = x_ref.shape[1]
    parts = [slice(i * (tm // OUT_SPLIT), (i + 1) * (tm // OUT_SPLIT)) for i in range(OUT_SPLIT)]

    def merge(rows):
        mr = mr_ref[0, rows, :]
        merged = None
        for n, y_ref in enumerate((ymla_ref, ysb_ref, yfox_ref, ymem_ref)):
            y = y_ref[0, :, rows].T
            gz = gate_ref[0, rows, n * BRANCH_WIDTH:(n + 1) * BRANCH_WIDTH].astype(F32)
            yb = (y * (gz * _sigmoid(gz))).astype(BF16)
            branch = _dot(yb, wb_ref[n])
            mg = _sigmoid(_dot(mr, wmu_ref[:, n * D_MODEL:(n + 1) * D_MODEL]))
            merged = mg * branch if merged is None else merged + mg * branch
        return merged.astype(BF16)

    merged = [merge(rows) for rows in parts]
    outs = [_dot(m, wo_ref[...]) for m in merged]
    for rows, out in zip(parts, outs):
        r = DEEPNORM_ALPHA * x_ref[0, rows, :] + out
        mu = jnp.mean(r, axis=-1, keepdims=True)
        var = jnp.mean(jnp.square(r - mu), axis=-1, keepdims=True)
        o_ref[0, rows, :] = (r - mu) * lax.rsqrt(var + LN_EPS) * g_ref[...] + b_ref[...]


def _out(x, l, ys, gate, mr, wmu, wb, wo, g, bias):
    b, s, d = x.shape
    lay = lambda a: _layer_spec(a, l)
    yt_spec = pl.BlockSpec((1, BRANCH_WIDTH, TM), lambda i, j: (i, 0, j))
    return pl.pallas_call(
        _out_body,
        grid=(b, s // TM),
        in_specs=[pl.BlockSpec((1, TM, d), lambda i, j: (i, j, 0)), yt_spec, yt_spec, yt_spec, yt_spec,
                  pl.BlockSpec((1, TM, N_BRANCHES * BRANCH_WIDTH), lambda i, j: (i, j, 0)),
                  pl.BlockSpec((1, TM, MERGE_RANK), lambda i, j: (i, j, 0)),
                  lay(wmu), lay(wb), lay(wo), lay(g), lay(bias)],
        out_specs=pl.BlockSpec((1, TM, d), lambda i, j: (i, j, 0)),
        out_shape=jax.ShapeDtypeStruct((b, s, d), F32),
        compiler_params=pltpu.CompilerParams(dimension_semantics=("arbitrary", "arbitrary"),
                                             vmem_limit_bytes=VMEM_LIMIT),
        name="fused_output",
    )(x, *ys, gate, mr, wmu, wb, wo, g, bias)


def kernel(x, mem, positions, w_in, mla_q_norm, mla_w_qb, mla_kv_norm, mla_w_kvb, fox_forget_bias, w_mem_kv,
           w_merge_up, w_branch, w_out, ln_gain, ln_bias):
    b, s, d = x.shape
    assert d == D_MODEL and s % TM == 0 and s % TQ == 0
    assert all(TQ % (2 * tk) == 0 and TM % tk == 0 for tk in (TK_SM, TK_SB)) and HB == HEADS
    depth = w_in.shape[0]

    cos_t, sin_t = _rope_tables(positions)

    mkv = _memkv(mem, w_mem_kv).reshape(depth, b, MEM_LEN, 2, HEADS, HEAD_V)
    kmem = mkv[:, :, :, 0].transpose(0, 1, 3, 2, 4).astype(BF16)
    vmemt = mkv[:, :, :, 1].transpose(0, 1, 3, 4, 2).astype(BF16)

    consts = _selection_constants()
    ntri = jnp.asarray(-np.triu(np.ones((TK_SB, TK_SB), np.float32)), BF16)
    weights = _proj_weights(w_in, mla_q_norm, mla_w_qb, mla_kv_norm, mla_w_kvb, fox_forget_bias)
    wmu, wb, wo = w_merge_up.astype(BF16), w_branch.astype(BF16), w_out.astype(BF16)
    ln_g, ln_b = ln_gain[:, None, :], ln_bias[:, None, :]
    whole = lambda a: pl.BlockSpec((1,) + a.shape[1:], lambda i, g, j: (i,) + (0,) * (a.ndim - 1))

    for l in range(depth):
        (qmla, kmla, vmla, qsb, ksb, vsb, qfox, kfox, vfox, ymem, gate, mr, kn, cend) = _proj(
            x, l, weights, consts, cos_t, sin_t, kmem, vmemt)
        ymla = _attention("mla_attention", functools.partial(_softmax_attn_body, decay=False), qmla, kmla, vmla,
                          TK_SM, HB)
        ysb = _attention("stick_breaking_attention", _sb_attn_body, qsb, ksb, vsb, TK_SB, HB,
                         extra=((ntri, pl.BlockSpec(ntri.shape, lambda i, g, j: (0, 0))),))
        yfox = _attention("forgetting_attention", functools.partial(_softmax_attn_body, decay=True), qfox, kfox, vfox,
                          TK_SM, HB, extra=((kn, whole(kn)), (cend, whole(cend))))
        x = _out(x, l, (ymla, ysb, yfox, ymem), gate, mr, wmu, wb, wo, ln_g, ln_b)
    return x
```

```python
import functools

import jax
import jax.numpy as jnp
import numpy as np
from jax import lax
from jax.experimental import pallas as pl
from jax.experimental.pallas import tpu as pltpu

F32, BF16 = jnp.float32, jnp.bfloat16

D_MODEL = 1024
DEPTH = 4
MEM_LEN = 256
N_BRANCHES = 4
BRANCH_WIDTH = 256
HEADS = 4
HEAD_V = 64
MLA_NOPE = 64
MLA_ROPE = 32
MLA_Q_RANK = 384
MLA_KV_RANK = 256
ROPE_THETA = 10000.0
SB_HEAD_DIM = 64
MERGE_RANK = 128
RMS_EPS = 1e-6
LN_EPS = 1e-5
DEEPNORM_ALPHA = (2 * DEPTH) ** 0.25

_C_CQ, _C_CKV, _C_KR = 0, 384, 640
_C_SQ, _C_SK, _C_SV = 672, 928, 1184
_C_FQ, _C_FK, _C_FV = 1440, 1696, 1952
_C_FF, _C_MQ, _C_GATE, _C_MR = 2208, 2212, 2468, 3492

HEAD_PAD = 128
TM = 512
TQ = 512
TK_SM = 256
TQ_SB = 256
TK_SB = 128
HB = 4
OUT_SPLIT = 4
ACC_ROWS = HEAD_V + 16
LOG2E = 1.4426950408889634
DEAD_LOG2 = -160.0
NEG = -1e30
VMEM_LIMIT = 56 * 1024 * 1024

_T_CQ, _T_CKV, _T_SQ, _T_SV, _T_FQ, _T_FV, _T_MQ, _T_FF, _T_END = 0, 384, 640, 896, 1152, 1408, 1664, 1920, 1936
_R_ROPE, _R_SK, _R_FK, _R_GATE, _R_MR, _R_END = 0, 128, 384, 896, 1920, 2048
AUG = 16


def _dot(a, b):
    return jnp.dot(a, b, preferred_element_type=F32)


def _dot_nt(a, b):
    return lax.dot_general(a, b, (((1,), (1,)), ((), ())), preferred_element_type=F32)


def _sigmoid(v):
    return 0.5 * jnp.tanh(0.5 * v) + 0.5


def _split3(v):
    p1 = v.astype(BF16).astype(F32)
    r1 = v - p1
    p2 = r1.astype(BF16).astype(F32)
    p3 = (r1 - p2).astype(BF16).astype(F32)
    return p1, p2, p3


def _rope_body(pos_ref, invf_ref, cos_ref, sin_ref):
    ang = pos_ref[0].astype(F32) * invf_ref[...]
    cos_ref[0] = jnp.cos(ang)
    sin_ref[0] = jnp.sin(ang)


def _rope_tables(positions):
    b, s = positions.shape
    half = MLA_ROPE // 2
    inv_freq = ROPE_THETA ** (-jnp.arange(0, MLA_ROPE, 2, dtype=F32) / MLA_ROPE)
    return pl.pallas_call(
        _rope_body,
        grid=(b,),
        in_specs=[pl.BlockSpec((1, 1, s), lambda i: (i, 0, 0)),
                  pl.BlockSpec((half, 1), lambda i: (0, 0))],
        out_specs=[pl.BlockSpec((1, half, s), lambda i: (i, 0, 0))] * 2,
        out_shape=[jax.ShapeDtypeStruct((b, half, s), F32)] * 2,
        name="rope_tables",
    )(positions.reshape(b, 1, s), inv_freq.reshape(half, 1))


def _memkv_body(mem_ref, w_ref, o_ref):
    o_ref[0] = _dot(mem_ref[...].astype(BF16), w_ref[0].astype(BF16))


def _memkv(mem, w_mem_kv):
    b, m, d = mem.shape
    depth, _, n = w_mem_kv.shape
    return pl.pallas_call(
        _memkv_body,
        grid=(depth,),
        in_specs=[pl.BlockSpec((b * m, d), lambda l: (0, 0)),
                  pl.BlockSpec((1, d, n), lambda l: (l, 0, 0))],
        out_specs=pl.BlockSpec((1, b * m, n), lambda l: (l, 0, 0)),
        out_shape=jax.ShapeDtypeStruct((depth, b * m, n), F32),
        compiler_params=pltpu.CompilerParams(vmem_limit_bytes=VMEM_LIMIT),
        name="mem_kv_proj",
    )(mem.reshape(b * m, d), w_mem_kv)


def _proj_body(x_ref, wt_ref, wr_ref, qn_ref, kvn_ref, wqbt_ref, wvt_ref, wkmla_ref,
               cos_ref, sin_ref, fbias_ref, selq_ref, onesq_ref, selk_ref, onesk_ref,
               kmem_ref, vmemt_ref,
               qmla_o, kmla_o, vmla_o, qsb_o, ksb_o, vsb_o, qfox_o, kfox_o, vfox_o, ymem_o, gate_o, mr_o,
               kn_o, cend_o, carry_ref):
    tm = x_ref.shape[1]
    xb = x_ref[0].astype(BF16)

    def store_vt(o_ref, vt):
        tk = o_ref.shape[3]
        for j in range(tm // tk):
            o_ref[0, j] = vt[:, j * tk:(j + 1) * tk].astype(BF16)

    def rms(v, g):
        return v * lax.rsqrt(jnp.mean(v * v, axis=0, keepdims=True) + RMS_EPS) * g

    ff = _dot_nt(wt_ref[_T_FF:_T_END, :], xb) + fbias_ref[...]
    lf = jnp.minimum(ff, 0.0) - jnp.log1p(jnp.exp(-jnp.abs(ff)))
    upper = (lax.broadcasted_iota(jnp.int32, (tm, tm), 0) <= lax.broadcasted_iota(jnp.int32, (tm, tm), 1))
    cqn = rms(_dot_nt(wt_ref[_T_CQ:_T_CKV, :], xb), qn_ref[...]).astype(BF16)
    ckvn = rms(_dot_nt(wt_ref[_T_CKV:_T_SQ, :], xb), kvn_ref[...])
    ckvn_t, ckvn_r = ckvn.astype(BF16), ckvn.T.astype(BF16)
    cos_t, sin_t = cos_ref[0], sin_ref[0]
    tab = jnp.concatenate([cos_t, cos_t, sin_t, sin_t, jnp.zeros((HEAD_PAD - 2 * MLA_ROPE, tm), F32)], axis=0).T
    prod = _dot(xb, wr_ref[:, _R_ROPE:_R_SK]) * tab
    lane = lax.broadcasted_iota(jnp.int32, prod.shape, 1)
    kpe = jnp.where(lane < MLA_ROPE, prod + pltpu.roll(prod, 128 - MLA_ROPE, 1), 0.0)
    kpe = pltpu.roll(kpe, MLA_NOPE, 1)
    mq = (_dot_nt(wt_ref[_T_MQ:_T_FF, :], xb) * (HEAD_V ** -0.5)).astype(BF16)

    csum = _dot(jnp.concatenate(_split3(lf), axis=0).astype(BF16), upper.astype(BF16))

    @pl.when(pl.program_id(1) == 0)
    def _():
        carry_ref[...] = jnp.zeros_like(carry_ref)

    c = csum[0:AUG] + csum[AUG:2 * AUG] + csum[2 * AUG:3 * AUG] + carry_ref[:, 0:1]
    carry_ref[...] = jnp.broadcast_to(c[:, tm - 1:tm], carry_ref.shape)
    c_log2 = c * LOG2E
    cst = jnp.concatenate([*_split3(c_log2), jnp.zeros((HEAD_PAD - 3 * AUG, tm), F32)], axis=0)
    tk_fox = vfox_o.shape[3]
    for j in range(tm // tk_fox):
        cend_o[0, j] = jnp.broadcast_to(c_log2[:, (j + 1) * tk_fox - 1:(j + 1) * tk_fox], cend_o.shape[2:])

    sq = _dot_nt(wt_ref[_T_SQ:_T_SV, :], xb) * (SB_HEAD_DIM ** -0.5 * LOG2E)
    zs = jnp.zeros((HEAD_PAD - SB_HEAD_DIM, tm), F32)
    for h in range(HEADS):
        qh = sq[h * SB_HEAD_DIM:(h + 1) * SB_HEAD_DIM]
        blk = jnp.concatenate([qh, zs] if h % 2 == 0 else [zs, qh], axis=0)
        qsb_o[0, h * HEAD_PAD:(h + 1) * HEAD_PAD, :] = blk.astype(BF16)
    q = _dot(wqbt_ref[...], cqn) * ((MLA_NOPE + MLA_ROPE) ** -0.5 * LOG2E)
    half = MLA_ROPE // 2
    dq = MLA_NOPE + MLA_ROPE
    zq = jnp.zeros((HEAD_PAD - dq, tm), F32)
    for h in range(HEADS):
        nope = q[h * dq:h * dq + MLA_NOPE]
        x1 = q[h * dq + MLA_NOPE:h * dq + MLA_NOPE + half]
        x2 = q[h * dq + MLA_NOPE + half:(h + 1) * dq]
        blk = jnp.concatenate([nope, x1 * cos_t - x2 * sin_t, x1 * sin_t + x2 * cos_t, zq], axis=0)
        qmla_o[0, h * HEAD_PAD:(h + 1) * HEAD_PAD, :] = blk.astype(BF16)

    store_vt(vsb_o, _dot_nt(wt_ref[_T_SV:_T_FQ, :], xb))
    store_vt(vmla_o, _dot(wvt_ref[...], ckvn_t))
    ksb_o[0] = _dot(xb, wr_ref[:, _R_SK:_R_FK]).astype(BF16)
    kmla_o[0] = (_dot(ckvn_r, wkmla_ref[...]) + jnp.concatenate([kpe] * HEADS, axis=1)).astype(BF16)

    fq = _dot_nt(wt_ref[_T_FQ:_T_FV, :], xb) * (SB_HEAD_DIM ** -0.5 * LOG2E)
    aug_t = _dot(selq_ref[...], cst.astype(BF16)) + onesq_ref[...]
    zf = jnp.zeros((HEAD_PAD - SB_HEAD_DIM - 8, tm), F32)
    for h in range(HEADS):
        blk = jnp.concatenate([fq[h * SB_HEAD_DIM:(h + 1) * SB_HEAD_DIM], aug_t[h * 8:(h + 1) * 8], zf], axis=0)
        qfox_o[0, h * HEAD_PAD:(h + 1) * HEAD_PAD, :] = blk.astype(BF16)
    store_vt(vfox_o, _dot_nt(wt_ref[_T_FV:_T_MQ, :], xb))
    kfox = _dot(xb, wr_ref[:, _R_FK:_R_GATE]) + _dot(cst.T.astype(BF16), selk_ref[...]) + onesk_ref[...]
    kfox = kfox.astype(BF16)
    kfox_o[0] = kfox
    ksq = jnp.square(kfox.astype(F32))
    group = lax.broadcasted_iota(jnp.int32, kn_o.shape[2:], 1) // 32
    kn = jnp.zeros(kn_o.shape[2:], F32)
    for h in range(HEADS):
        n2 = jnp.sum(ksq[:, h * HEAD_PAD:h * HEAD_PAD + SB_HEAD_DIM], axis=1, keepdims=True)
        kn = jnp.where(group == h, jnp.max(n2, axis=0, keepdims=True), kn)
    kn_o[0, 0] = kn

    s_mem = [_dot(kmem_ref[0, h], mq[h * HEAD_V:(h + 1) * HEAD_V]) for h in range(HEADS)]
    gate_o[0] = _dot(xb, wr_ref[:, _R_GATE:_R_MR]).astype(BF16)
    mr_o[0] = _dot(xb, wr_ref[:, _R_MR:_R_END]).astype(BF16)
    for h in range(HEADS):
        p = jnp.exp(s_mem[h] - jnp.max(s_mem[h], axis=0, keepdims=True))
        o = _dot(vmemt_ref[0, h], p.astype(BF16))
        ymem_o[0, h * HEAD_V:(h + 1) * HEAD_V, :] = o / jnp.sum(p, axis=0, keepdims=True)


def _placed(w, n_in):
    depth, d, _ = w.shape
    w = w.reshape(depth, d, HEADS, n_in)
    return jnp.pad(w, ((0, 0), (0, 0), (0, 0), (0, HEAD_PAD - n_in))).reshape(depth, d, HEADS * HEAD_PAD)


def _selection_constants():
    selq = np.zeros((HEADS * 8, HEAD_PAD), np.float32)
    onesq = np.zeros((HEADS * 8, 1), np.float32)
    selk = np.zeros((HEAD_PAD, HEADS * HEAD_PAD), np.float32)
    onesk = np.zeros((1, HEADS * HEAD_PAD), np.float32)
    for h in range(HEADS):
        for i in range(3):
            selq[h * 8 + i, i * AUG + h] = 1.0
            onesq[h * 8 + 3 + i, 0] = 1.0
            onesk[0, h * HEAD_PAD + SB_HEAD_DIM + i] = 1.0
            selk[i * AUG + h, h * HEAD_PAD + SB_HEAD_DIM + 3 + i] = -1.0
    return jnp.asarray(selq, BF16), jnp.asarray(onesq), jnp.asarray(selk, BF16), jnp.asarray(onesk)


def _proj_weights(w_in, q_norm, w_qb, kv_norm, w_kvb, fox_bias):
    depth = w_in.shape[0]
    cols = lambda a, n: w_in[:, :, a:a + n]
    t = lambda a: jnp.swapaxes(a, 1, 2)
    wt = jnp.concatenate([
        t(cols(_C_CQ, 384)), t(cols(_C_CKV, 256)), t(cols(_C_SQ, 256)), t(cols(_C_SV, 256)),
        t(cols(_C_FQ, 256)), t(cols(_C_FV, 256)), t(cols(_C_MQ, 256)), t(cols(_C_FF, 4)),
        jnp.zeros((depth, _T_END - _T_FF - HEADS, D_MODEL), F32)], axis=1).astype(BF16)
    kr = cols(_C_KR, MLA_ROPE)
    half = MLA_ROPE // 2
    rope = jnp.concatenate([kr, -kr[:, :, half:], kr[:, :, :half],
                            jnp.zeros((depth, D_MODEL, HEAD_PAD - 2 * MLA_ROPE), F32)], axis=2)
    wr = jnp.concatenate([
        rope, cols(_C_SK, 256), _placed(cols(_C_FK, 256), SB_HEAD_DIM),
        cols(_C_GATE, 1024), cols(_C_MR, MERGE_RANK)], axis=2).astype(BF16)
    kvb = w_kvb.reshape(depth, MLA_KV_RANK, HEADS, MLA_NOPE + HEAD_V)
    wkmla = jnp.pad(kvb[..., :MLA_NOPE], ((0, 0), (0, 0), (0, 0), (0, HEAD_PAD - MLA_NOPE)))
    wkmla = wkmla.reshape(depth, MLA_KV_RANK, HEADS * HEAD_PAD)
    wvt = t(kvb[..., MLA_NOPE:].reshape(depth, MLA_KV_RANK, HEADS * HEAD_V))
    fbias = jnp.pad(fox_bias, ((0, 0), (0, AUG - HEADS)))[:, :, None]
    return (wt, wr, q_norm[:, :, None], kv_norm[:, :, None],
            t(w_qb).astype(BF16), wvt.astype(BF16), wkmla.astype(BF16), fbias)


def _layer_spec(a, l):
    return pl.BlockSpec((None,) + a.shape[1:], lambda i, j: (l,) + (0,) * (a.ndim - 1))


def _proj(x, l, weights, consts, cos_t, sin_t, kmem, vmemt):
    b, s, d = x.shape
    lay = lambda a: _layer_spec(a, l)
    wt, wr, qn, kvn, wqbt, wvt, wkmla, fbias = weights
    selq, onesq, selk, onesk = consts
    full = lambda a: pl.BlockSpec(a.shape, lambda i, j: (0,) * a.ndim)
    qt_spec = pl.BlockSpec((1, HEADS * HEAD_PAD, TM), lambda i, j: (i, 0, j))
    k_spec = lambda w: pl.BlockSpec((1, TM, w), lambda i, j: (i, j, 0))
    vt_spec = lambda tk: pl.BlockSpec((1, TM // tk, HEADS * HEAD_V, tk), lambda i, j: (i, j, 0, 0))
    qt_shape = jax.ShapeDtypeStruct((b, HEADS * HEAD_PAD, s), BF16)
    k_shape = lambda w: jax.ShapeDtypeStruct((b, s, w), BF16)
    kw, kw_sb = HEADS * HEAD_PAD, HEADS * SB_HEAD_DIM
    vt_shape = lambda tk: jax.ShapeDtypeStruct((b, s // tk, HEADS * HEAD_V, tk), BF16)
    return pl.pallas_call(
        _proj_body,
        grid=(b, s // TM),
        in_specs=[pl.BlockSpec((1, TM, d), lambda i, j: (i, j, 0)),
                  lay(wt), lay(wr), lay(qn), lay(kvn), lay(wqbt), lay(wvt), lay(wkmla),
                  pl.BlockSpec((1, MLA_ROPE // 2, TM), lambda i, j: (i, 0, j)),
                  pl.BlockSpec((1, MLA_ROPE // 2, TM), lambda i, j: (i, 0, j)),
                  lay(fbias), full(selq), full(onesq), full(selk), full(onesk),
                  pl.BlockSpec((None, 1, HEADS, MEM_LEN, HEAD_V), lambda i, j: (l, i, 0, 0, 0)),
                  pl.BlockSpec((None, 1, HEADS, HEAD_V, MEM_LEN), lambda i, j: (l, i, 0, 0, 0))],
        out_specs=[qt_spec, k_spec(kw), vt_spec(TK_SM), qt_spec, k_spec(kw_sb), vt_spec(TK_SB), qt_spec, k_spec(kw),
                   vt_spec(TK_SM),
                   pl.BlockSpec((1, HEADS * HEAD_V, TM), lambda i, j: (i, 0, j)),
                   pl.BlockSpec((1, TM, N_BRANCHES * BRANCH_WIDTH), lambda i, j: (i, j, 0)),
                   pl.BlockSpec((1, TM, MERGE_RANK), lambda i, j: (i, j, 0)),
                   pl.BlockSpec((1, 1, 8, 128), lambda i, j: (i, j, 0, 0)),
                   pl.BlockSpec((1, TM // TK_SM, AUG, 128), lambda i, j: (i, j, 0, 0))],
        out_shape=[qt_shape, k_shape(kw), vt_shape(TK_SM), qt_shape, k_shape(kw_sb), vt_shape(TK_SB), qt_shape,
                   k_shape(kw), vt_shape(TK_SM),
                   jax.ShapeDtypeStruct((b, HEADS * HEAD_V, s), F32),
                   jax.ShapeDtypeStruct((b, s, N_BRANCHES * BRANCH_WIDTH), BF16),
                   jax.ShapeDtypeStruct((b, s, MERGE_RANK), BF16),
                   jax.ShapeDtypeStruct((b, s // TM, 8, 128), F32),
                   jax.ShapeDtypeStruct((b, s // TK_SM, AUG, 128), F32)],
        scratch_shapes=[pltpu.VMEM((AUG, 128), F32)],
        compiler_params=pltpu.CompilerParams(dimension_semantics=("arbitrary", "arbitrary"),
                                             vmem_limit_bytes=VMEM_LIMIT),
        name="fused_projection",
    )(x, wt, wr, qn, kvn, wqbt, wvt, wkmla, cos_t, sin_t, fbias, selq, onesq, selk, onesk, kmem, vmemt)


def _causal_masks(tq, tk, strict):
    row = lax.broadcasted_iota(jnp.int32, (tk, tq), 0)
    col = lax.broadcasted_iota(jnp.int32, (tk, tq), 1)
    return [(row + j * tk < col) if strict else (row + j * tk <= col) for j in range(tq // tk)]


def _softmax_attn_body(q_ref, k_ref, v_ref, *rest, tq, tk, hb, decay):
    if decay:
        kn_ref, cend_ref, o_ref, acc_ref, s_ref = rest
    else:
        o_ref, acc_ref, s_ref = rest
    nd = tq // tk
    qi = pl.program_id(2)
    n_full = qi * nd
    acc_ref[...] = jnp.zeros_like(acc_ref)

    def scores(k, h):
        rows = pl.ds(pl.multiple_of(k * tk, tk), tk)
        return _dot(k_ref[0, rows, h * HEAD_PAD:(h + 1) * HEAD_PAD], q_ref[0, h * HEAD_PAD:(h + 1) * HEAD_PAD, :])

    ones = jnp.ones((ACC_ROWS - HEAD_V, tk), BF16)

    def step(k, slot, carry, mask):
        k_next = jnp.maximum(k - 1, 0)
        out = [None] * hb

        def prefetch(h):
            if h < hb:
                s_ref[1 - slot, h] = scores(k_next, h)

        def softmax_pv(h):
            m = carry[h]
            s = s_ref[slot, h]
            if mask is not None:
                s = jnp.where(mask, s, NEG)
            m_new = jnp.maximum(m, jnp.max(s, axis=0, keepdims=True))
            p = jnp.exp2(s - m_new)
            if mask is not None:
                p = jnp.where(mask, p, 0.0)
            v1 = jnp.concatenate([v_ref[0, k, h * HEAD_V:(h + 1) * HEAD_V, :], ones], axis=0)
            ha = slice(h * ACC_ROWS, (h + 1) * ACC_ROWS)
            acc_ref[ha, :] = jnp.exp2(m - m_new) * acc_ref[ha, :] + _dot(v1, p.astype(BF16))
            out[h] = m_new

        prefetch(0)
        for h in range(hb):
            prefetch(h + 1)
            softmax_pv(h)
        return tuple(out)

    for h in range(hb):
        s_ref[(nd - 1) % 2, h] = scores(n_full + nd - 1, h)
    carry = tuple(jnp.full((1, tq), NEG, F32) for _ in range(hb))
    masks = _causal_masks(tq, tk, strict=False)
    for j in reversed(range(nd)):
        carry = step(n_full + j, j % 2, carry, masks[j])

    def pair(i, carry):
        k = n_full - 1 - 2 * i
        carry = step(k, 1, carry, None)
        return step(k - 1, 0, carry, None)

    if not decay:
        carry = lax.fori_loop(0, n_full // 2, pair, carry)
    else:
        assert hb == HEADS
        kmax = jnp.sqrt(jnp.max(kn_ref[0], axis=0)[0:1, :]) * 1.01
        reach = []
        for h in range(hb):
            qh = q_ref[0, h * HEAD_PAD:h * HEAD_PAD + SB_HEAD_DIM + 8, :].astype(F32)
            qnorm = jnp.sqrt(jnp.sum(jnp.square(qh[0:SB_HEAD_DIM]), axis=0, keepdims=True))
            c_t = jnp.sum(qh[SB_HEAD_DIM:SB_HEAD_DIM + 3], axis=0, keepdims=True)
            reach.append(qnorm * kmax[:, 32 * h:32 * h + 1] + c_t)

        def alive(carry, k_low):
            c_end = cend_ref[0, jnp.maximum(k_low - 1, 0)]
            top = None
            for h in range(hb):
                gap = reach[h] - c_end[h:h + 1, 0:1] - carry[h]
                top = gap if top is None else jnp.maximum(top, gap)
            return jnp.max(top) > DEAD_LOG2

        def cond(state):
            i, live, _ = state
            return jnp.logical_and(i < n_full // 2, live)

        def body(state):
            i, _, carry = state
            carry = pair(i, carry)
            return i + 1, alive(carry, n_full - 2 * (i + 1)), carry

        carry = lax.while_loop(cond, body, (jnp.int32(0), alive(carry, n_full), carry))[2]
    for h in range(hb):
        num = acc_ref[h * ACC_ROWS:h * ACC_ROWS + HEAD_V, :]
        den = acc_ref[h * ACC_ROWS + HEAD_V:h * ACC_ROWS + HEAD_V + 1, :]
        o_ref[0, h * HEAD_V:(h + 1) * HEAD_V, :] = num * (1.0 / den)


def _sb_attn_body(q_ref, k_ref, v_ref, ntri_ref, o_ref, acc_ref, s_ref, *, tq, tk, hb):
    nd = tq // tk
    qi = pl.program_id(2)
    n_full = qi * nd
    acc_ref[...] = jnp.zeros_like(acc_ref)

    def scores(k, h):
        rows = pl.ds(pl.multiple_of(k * tk, tk), tk)
        pair_lanes = slice((h // 2) * HEAD_PAD, (h // 2 + 1) * HEAD_PAD)
        return _dot(k_ref[0, rows, pair_lanes], q_ref[0, h * HEAD_PAD:(h + 1) * HEAD_PAD, :])

    def step(k, slot, carry, mask):
        k_next = jnp.maximum(k - 1, 0)
        out = [None] * hb
        local = [None] * hb

        def suffix(h):
            if h >= hb:
                return
            z = s_ref[slot, h]
            neg_abs = pltpu.bitcast(pltpu.bitcast(z, jnp.uint32) | jnp.uint32(0x80000000), F32)
            sp = jnp.maximum(z, 0.0) + jnp.log(1.0 + jnp.exp2(neg_abs)) * LOG2E
            if mask is not None:
                sp = jnp.where(mask, sp, 0.0)
            local[h] = _dot(ntri_ref[...], sp.astype(BF16))
            s_ref[1 - slot, h] = scores(k_next, h)

        def weights_pv(h):
            cross = carry[h]
            a = jnp.exp2(s_ref[slot, h] + local[h])
            if mask is not None:
                a = jnp.where(mask, a, 0.0)
            hv = slice(h * HEAD_V, (h + 1) * HEAD_V)
            acc_ref[hv, :] += _dot(v_ref[0, k, hv, :], a.astype(BF16)) * jnp.exp2(cross)
            out[h] = cross + local[h][0:1, :]

        suffix(0)
        for h in range(hb):
            suffix(h + 1)
            weights_pv(h)
        return tuple(out)

    for h in range(hb):
        s_ref[(nd - 1) % 2, h] = scores(n_full + nd - 1, h)
    carry = tuple(jnp.zeros((1, tq), F32) for _ in range(hb))
    masks = _causal_masks(tq, tk, strict=True)
    for j in reversed(range(nd)):
        carry = step(n_full + j, j % 2, carry, masks[j])

    def pair(i, carry):
        k = n_full - 1 - 2 * i
        carry = step(k, 1, carry, None)
        return step(k - 1, 0, carry, None)

    def alive(carry):
        top = functools.reduce(jnp.maximum, carry)
        return jnp.max(top) > DEAD_LOG2

    def cond(state):
        i, live, _ = state
        return jnp.logical_and(i < n_full // 2, live)

    def body(state):
        i, _, carry = state
        carry = pair(i, carry)
        return i + 1, alive(carry), carry

    lax.while_loop(cond, body, (jnp.int32(0), alive(carry), carry))
    o_ref[0] = acc_ref[...]


def _attention(name, body, q_t, k, v_t, tq, tk, hb, acc_rows, extra=()):
    b, _, s = q_t.shape
    in_specs = [pl.BlockSpec((1, hb * HEAD_PAD, tq), lambda i, g, j: (i, g, j)),
                pl.BlockSpec((1, s, k.shape[2]), lambda i, g, j: (i, 0, g)),
                pl.BlockSpec((1, s // tk, hb * HEAD_V, tk), lambda i, g, j: (i, 0, g, 0))]
    in_specs += [spec for _, spec in extra]
    return pl.pallas_call(
        functools.partial(body, tq=tq, tk=tk, hb=hb),
        grid=(b, HEADS // hb, s // tq),
        in_specs=in_specs,
        out_specs=pl.BlockSpec((1, hb * HEAD_V, tq), lambda i, g, j: (i, g, j)),
        out_shape=jax.ShapeDtypeStruct((b, HEADS * HEAD_V, s), F32),
        scratch_shapes=[pltpu.VMEM((hb * acc_rows, tq), F32), pltpu.VMEM((2, hb, tk, tq), F32)],
        compiler_params=pltpu.CompilerParams(dimension_semantics=("arbitrary",) * 3,
                                             vmem_limit_bytes=VMEM_LIMIT),
        name=name,
    )(q_t, k, v_t, *[a for a, _ in extra])


def _out_body(x_ref, ymla_ref, ysb_ref, yfox_ref, ymem_ref, gate_ref, mr_ref, wmu_ref, wb_ref, wo_ref,
              g_ref, b_ref, o_ref):
    tm = x_ref.shape[1]
    parts = [slice(i * (tm // OUT_SPLIT), (i + 1) * (tm // OUT_SPLIT)) for i in range(OUT_SPLIT)]

    def merge(rows):
        mr = mr_ref[0, rows, :]
        merged = None
        for n, y_ref in enumerate((ymla_ref, ysb_ref, yfox_ref, ymem_ref)):
            y = y_ref[0, :, rows].T
            gz = gate_ref[0, rows, n * BRANCH_WIDTH:(n + 1) * BRANCH_WIDTH].astype(F32)
            yb = (y * (gz * _sigmoid(gz))).astype(BF16)
            branch = _dot(yb, wb_ref[n])
            mg = _sigmoid(_dot(mr, wmu_ref[:, n * D_MODEL:(n + 1) * D_MODEL]))
            merged = mg * branch if merged is None else merged + mg * branch
        return merged.astype(BF16)

    merged = [merge(rows) for rows in parts]
    outs = [_dot(m, wo_ref[...]) for m in merged]
    for rows, out in zip(parts, outs):
        r = DEEPNORM_ALPHA * x_ref[0, rows, :] + out
        mu = jnp.mean(r, axis=-1, keepdims=True)
        var = jnp.mean(jnp.square(r - mu), axis=-1, keepdims=True)
        o_ref[0, rows, :] = (r - mu) * lax.rsqrt(var + LN_EPS) * g_ref[...] + b_ref[...]


def _out(x, l, ys, gate, mr, wmu, wb, wo, g, bias):
    b, s, d = x.shape
    lay = lambda a: _layer_spec(a, l)
    yt_spec = pl.BlockSpec((1, BRANCH_WIDTH, TM), lambda i, j: (i, 0, j))
    return pl.pallas_call(
        _out_body,
        grid=(b, s // TM),
        in_specs=[pl.BlockSpec((1, TM, d), lambda i, j: (i, j, 0)), yt_spec, yt_spec, yt_spec, yt_spec,
                  pl.BlockSpec((1, TM, N_BRANCHES * BRANCH_WIDTH), lambda i, j: (i, j, 0)),
                  pl.BlockSpec((1, TM, MERGE_RANK), lambda i, j: (i, j, 0)),
                  lay(wmu), lay(wb), lay(wo), lay(g), lay(bias)],
        out_specs=pl.BlockSpec((1, TM, d), lambda i, j: (i, j, 0)),
        out_shape=jax.ShapeDtypeStruct((b, s, d), F32),
        compiler_params=pltpu.CompilerParams(dimension_semantics=("arbitrary", "arbitrary"),
                                             vmem_limit_bytes=VMEM_LIMIT),
        name="fused_output",
    )(x, *ys, gate, mr, wmu, wb, wo, g, bias)


def kernel(x, mem, positions, w_in, mla_q_norm, mla_w_qb, mla_kv_norm, mla_w_kvb, fox_forget_bias, w_mem_kv,
           w_merge_up, w_branch, w_out, ln_gain, ln_bias):
    b, s, d = x.shape
    assert d == D_MODEL and s % TM == 0 and s % TQ == 0
    assert all(tq % (2 * tk) == 0 and TM % tk == 0 and s % tq == 0 for tq, tk in ((TQ, TK_SM), (TQ_SB, TK_SB)))
    assert HB == HEADS
    depth = w_in.shape[0]

    cos_t, sin_t = _rope_tables(positions)

    mkv = _memkv(mem, w_mem_kv).reshape(depth, b, MEM_LEN, 2, HEADS, HEAD_V)
    kmem = mkv[:, :, :, 0].transpose(0, 1, 3, 2, 4).astype(BF16)
    vmemt = mkv[:, :, :, 1].transpose(0, 1, 3, 4, 2).astype(BF16)

    consts = _selection_constants()
    ntri = jnp.asarray(-np.triu(np.ones((TK_SB, TK_SB), np.float32)), BF16)
    weights = _proj_weights(w_in, mla_q_norm, mla_w_qb, mla_kv_norm, mla_w_kvb, fox_forget_bias)
    wmu, wb, wo = w_merge_up.astype(BF16), w_branch.astype(BF16), w_out.astype(BF16)
    ln_g, ln_b = ln_gain[:, None, :], ln_bias[:, None, :]
    whole = lambda a: pl.BlockSpec((1,) + a.shape[1:], lambda i, g, j: (i,) + (0,) * (a.ndim - 1))

    for l in range(depth):
        (qmla, kmla, vmla, qsb, ksb, vsb, qfox, kfox, vfox, ymem, gate, mr, kn, cend) = _proj(
            x, l, weights, consts, cos_t, sin_t, kmem, vmemt)
        ymla = _attention("mla_attention", functools.partial(_softmax_attn_body, decay=False), qmla, kmla, vmla,
                          TQ, TK_SM, HB, ACC_ROWS)
        ysb = _attention("stick_breaking_attention", _sb_attn_body, qsb, ksb, vsb, TQ_SB, TK_SB, HB, HEAD_V,
                         extra=((ntri, pl.BlockSpec(ntri.shape, lambda i, g, j: (0, 0))),))
        yfox = _attention("forgetting_attention", functools.partial(_softmax_attn_body, decay=True), qfox, kfox, vfox,
                          TQ, TK_SM, HB, ACC_ROWS, extra=((kn, whole(kn)), (cend, whole(cend))))
        x = _out(x, l, (ymla, ysb, yfox, ymem), gate, mr, wmu, wb, wo, ln_g, ln_b)
    return x
```

```python
import functools

import jax
import jax.numpy as jnp
import numpy as np
from jax import lax
from jax.experimental import pallas as pl
from jax.experimental.pallas import tpu as pltpu

F32, BF16 = jnp.float32, jnp.bfloat16

D_MODEL = 1024
DEPTH = 4
MEM_LEN = 256
N_BRANCHES = 4
BRANCH_WIDTH = 256
HEADS = 4
HEAD_V = 64
MLA_NOPE = 64
MLA_ROPE = 32
MLA_Q_RANK = 384
MLA_KV_RANK = 256
ROPE_THETA = 10000.0
SB_HEAD_DIM = 64
MERGE_RANK = 128
RMS_EPS = 1e-6
LN_EPS = 1e-5
DEEPNORM_ALPHA = (2 * DEPTH) ** 0.25

_C_CQ, _C_CKV, _C_KR = 0, 384, 640
_C_SQ, _C_SK, _C_SV = 672, 928, 1184
_C_FQ, _C_FK, _C_FV = 1440, 1696, 1952
_C_FF, _C_MQ, _C_GATE, _C_MR = 2208, 2212, 2468, 3492

HEAD_PAD = 128
TM = 1024
TQ = 512
TK_SM = 256
TQ_SB = 256
TK_SB = 128
HB = 4
OUT_SPLIT = 8
ACC_ROWS = HEAD_V + 16
LOG2E = 1.4426950408889634
DEAD_LOG2 = -160.0
NEG = -1e30
VMEM_LIMIT = 56 * 1024 * 1024

_T_CQ, _T_CKV, _T_SQ, _T_SV, _T_FQ, _T_FV, _T_MQ, _T_FF, _T_END = 0, 384, 640, 896, 1152, 1408, 1664, 1920, 1936
_R_ROPE, _R_SK, _R_FK, _R_GATE, _R_MR, _R_END = 0, 128, 384, 896, 1920, 2048
AUG = 16


def _dot(a, b):
    return jnp.dot(a, b, preferred_element_type=F32)


def _dot_nt(a, b):
    return lax.dot_general(a, b, (((1,), (1,)), ((), ())), preferred_element_type=F32)


def _sigmoid(v):
    return 0.5 * jnp.tanh(0.5 * v) + 0.5


def _split3(v):
    p1 = v.astype(BF16).astype(F32)
    r1 = v - p1
    p2 = r1.astype(BF16).astype(F32)
    p3 = (r1 - p2).astype(BF16).astype(F32)
    return p1, p2, p3


def _rope_body(pos_ref, invf_ref, cos_ref, sin_ref):
    ang = pos_ref[0].astype(F32) * invf_ref[...]
    cos_ref[0] = jnp.cos(ang)
    sin_ref[0] = jnp.sin(ang)


def _rope_tables(positions):
    b, s = positions.shape
    half = MLA_ROPE // 2
    inv_freq = ROPE_THETA ** (-jnp.arange(0, MLA_ROPE, 2, dtype=F32) / MLA_ROPE)
    return pl.pallas_call(
        _rope_body,
        grid=(b,),
        in_specs=[pl.BlockSpec((1, 1, s), lambda i: (i, 0, 0)),
                  pl.BlockSpec((half, 1), lambda i: (0, 0))],
        out_specs=[pl.BlockSpec((1, half, s), lambda i: (i, 0, 0))] * 2,
        out_shape=[jax.ShapeDtypeStruct((b, half, s), F32)] * 2,
        name="rope_tables",
    )(positions.reshape(b, 1, s), inv_freq.reshape(half, 1))


def _memkv_body(mem_ref, w_ref, o_ref):
    o_ref[0] = _dot(mem_ref[...].astype(BF16), w_ref[0].astype(BF16))


def _memkv(mem, w_mem_kv):
    b, m, d = mem.shape
    depth, _, n = w_mem_kv.shape
    return pl.pallas_call(
        _memkv_body,
        grid=(depth,),
        in_specs=[pl.BlockSpec((b * m, d), lambda l: (0, 0)),
                  pl.BlockSpec((1, d, n), lambda l: (l, 0, 0))],
        out_specs=pl.BlockSpec((1, b * m, n), lambda l: (l, 0, 0)),
        out_shape=jax.ShapeDtypeStruct((depth, b * m, n), F32),
        compiler_params=pltpu.CompilerParams(vmem_limit_bytes=VMEM_LIMIT),
        name="mem_kv_proj",
    )(mem.reshape(b * m, d), w_mem_kv)


def _proj_body(x_ref, wt_ref, wr_ref, qn_ref, kvn_ref, wqbt_ref, wvt_ref, wkmla_ref,
               cos_ref, sin_ref, fbias_ref, selq_ref, onesq_ref, selk_ref, onesk_ref,
               kmem_ref, vmemt_ref,
               qmla_o, kmla_o, vmla_o, qsb_o, ksb_o, vsb_o, qfox_o, kfox_o, vfox_o, ymem_o, gate_o, mr_o,
               kn_o, cend_o, carry_ref):
    tm = x_ref.shape[1]
    xb = x_ref[0].astype(BF16)

    def store_vt(o_ref, vt):
        tk = o_ref.shape[3]
        for j in range(tm // tk):
            o_ref[0, j] = vt[:, j * tk:(j + 1) * tk].astype(BF16)

    def rms(v, g):
        return v * lax.rsqrt(jnp.mean(v * v, axis=0, keepdims=True) + RMS_EPS) * g

    ff = _dot_nt(wt_ref[_T_FF:_T_END, :], xb) + fbias_ref[...]
    lf = jnp.minimum(ff, 0.0) - jnp.log1p(jnp.exp(-jnp.abs(ff)))
    upper = (lax.broadcasted_iota(jnp.int32, (tm, tm), 0) <= lax.broadcasted_iota(jnp.int32, (tm, tm), 1))
    cqn = rms(_dot_nt(wt_ref[_T_CQ:_T_CKV, :], xb), qn_ref[...]).astype(BF16)
    ckvn = rms(_dot_nt(wt_ref[_T_CKV:_T_SQ, :], xb), kvn_ref[...])
    ckvn_t, ckvn_r = ckvn.astype(BF16), ckvn.T.astype(BF16)
    cos_t, sin_t = cos_ref[0], sin_ref[0]
    tab = jnp.concatenate([cos_t, cos_t, sin_t, sin_t, jnp.zeros((HEAD_PAD - 2 * MLA_ROPE, tm), F32)], axis=0).T
    prod = _dot(xb, wr_ref[:, _R_ROPE:_R_SK]) * tab
    lane = lax.broadcasted_iota(jnp.int32, prod.shape, 1)
    kpe = jnp.where(lane < MLA_ROPE, prod + pltpu.roll(prod, 128 - MLA_ROPE, 1), 0.0)
    kpe = pltpu.roll(kpe, MLA_NOPE, 1)
    mq = (_dot_nt(wt_ref[_T_MQ:_T_FF, :], xb) * (HEAD_V ** -0.5)).astype(BF16)

    csum = _dot(jnp.concatenate(_split3(lf), axis=0).astype(BF16), upper.astype(BF16))

    @pl.when(pl.program_id(1) == 0)
    def _():
        carry_ref[...] = jnp.zeros_like(carry_ref)

    c = csum[0:AUG] + csum[AUG:2 * AUG] + csum[2 * AUG:3 * AUG] + carry_ref[:, 0:1]
    carry_ref[...] = jnp.broadcast_to(c[:, tm - 1:tm], carry_ref.shape)
    c_log2 = c * LOG2E
    cst = jnp.concatenate([*_split3(c_log2), jnp.zeros((HEAD_PAD - 3 * AUG, tm), F32)], axis=0)
    tk_fox = vfox_o.shape[3]
    for j in range(tm // tk_fox):
        cend_o[0, j] = jnp.broadcast_to(c_log2[:, (j + 1) * tk_fox - 1:(j + 1) * tk_fox], cend_o.shape[2:])

    sq = _dot_nt(wt_ref[_T_SQ:_T_SV, :], xb) * (SB_HEAD_DIM ** -0.5 * LOG2E)
    zs = jnp.zeros((HEAD_PAD - SB_HEAD_DIM, tm), F32)
    for h in range(HEADS):
        qh = sq[h * SB_HEAD_DIM:(h + 1) * SB_HEAD_DIM]
        blk = jnp.concatenate([qh, zs] if h % 2 == 0 else [zs, qh], axis=0)
        qsb_o[0, h * HEAD_PAD:(h + 1) * HEAD_PAD, :] = blk.astype(BF16)
    q = _dot(wqbt_ref[...], cqn) * ((MLA_NOPE + MLA_ROPE) ** -0.5 * LOG2E)
    half = MLA_ROPE // 2
    dq = MLA_NOPE + MLA_ROPE
    zq = jnp.zeros((HEAD_PAD - dq, tm), F32)
    for h in range(HEADS):
        nope = q[h * dq:h * dq + MLA_NOPE]
        x1 = q[h * dq + MLA_NOPE:h * dq + MLA_NOPE + half]
        x2 = q[h * dq + MLA_NOPE + half:(h + 1) * dq]
        blk = jnp.concatenate([nope, x1 * cos_t - x2 * sin_t, x1 * sin_t + x2 * cos_t, zq], axis=0)
        qmla_o[0, h * HEAD_PAD:(h + 1) * HEAD_PAD, :] = blk.astype(BF16)

    store_vt(vsb_o, _dot_nt(wt_ref[_T_SV:_T_FQ, :], xb))
    store_vt(vmla_o, _dot(wvt_ref[...], ckvn_t))
    ksb_o[0] = _dot(xb, wr_ref[:, _R_SK:_R_FK]).astype(BF16)
    kmla_o[0] = (_dot(ckvn_r, wkmla_ref[...]) + jnp.concatenate([kpe] * HEADS, axis=1)).astype(BF16)

    fq = _dot_nt(wt_ref[_T_FQ:_T_FV, :], xb) * (SB_HEAD_DIM ** -0.5 * LOG2E)
    aug_t = _dot(selq_ref[...], cst.astype(BF16)) + onesq_ref[...]
    zf = jnp.zeros((HEAD_PAD - SB_HEAD_DIM - 8, tm), F32)
    for h in range(HEADS):
        blk = jnp.concatenate([fq[h * SB_HEAD_DIM:(h + 1) * SB_HEAD_DIM], aug_t[h * 8:(h + 1) * 8], zf], axis=0)
        qfox_o[0, h * HEAD_PAD:(h + 1) * HEAD_PAD, :] = blk.astype(BF16)
    store_vt(vfox_o, _dot_nt(wt_ref[_T_FV:_T_MQ, :], xb))
    kfox = _dot(xb, wr_ref[:, _R_FK:_R_GATE]) + _dot(cst.T.astype(BF16), selk_ref[...]) + onesk_ref[...]
    kfox = kfox.astype(BF16)
    kfox_o[0] = kfox
    ksq = jnp.square(kfox.astype(F32))
    group = lax.broadcasted_iota(jnp.int32, kn_o.shape[2:], 1) // 32
    kn = jnp.zeros(kn_o.shape[2:], F32)
    for h in range(HEADS):
        n2 = jnp.sum(ksq[:, h * HEAD_PAD:h * HEAD_PAD + SB_HEAD_DIM], axis=1, keepdims=True)
        kn = jnp.where(group == h, jnp.max(n2, axis=0, keepdims=True), kn)
    kn_o[0, 0] = kn

    s_mem = [_dot(kmem_ref[0, h], mq[h * HEAD_V:(h + 1) * HEAD_V]) for h in range(HEADS)]
    gate_o[0] = _dot(xb, wr_ref[:, _R_GATE:_R_MR]).astype(BF16)
    mr_o[0] = _dot(xb, wr_ref[:, _R_MR:_R_END]).astype(BF16)
    for h in range(HEADS):
        p = jnp.exp(s_mem[h] - jnp.max(s_mem[h], axis=0, keepdims=True))
        o = _dot(vmemt_ref[0, h], p.astype(BF16))
        ymem_o[0, h * HEAD_V:(h + 1) * HEAD_V, :] = o / jnp.sum(p, axis=0, keepdims=True)


def _placed(w, n_in):
    depth, d, _ = w.shape
    w = w.reshape(depth, d, HEADS, n_in)
    return jnp.pad(w, ((0, 0), (0, 0), (0, 0), (0, HEAD_PAD - n_in))).reshape(depth, d, HEADS * HEAD_PAD)


def _selection_constants():
    selq = np.zeros((HEADS * 8, HEAD_PAD), np.float32)
    onesq = np.zeros((HEADS * 8, 1), np.float32)
    selk = np.zeros((HEAD_PAD, HEADS * HEAD_PAD), np.float32)
    onesk = np.zeros((1, HEADS * HEAD_PAD), np.float32)
    for h in range(HEADS):
        for i in range(3):
            selq[h * 8 + i, i * AUG + h] = 1.0
            onesq[h * 8 + 3 + i, 0] = 1.0
            onesk[0, h * HEAD_PAD + SB_HEAD_DIM + i] = 1.0
            selk[i * AUG + h, h * HEAD_PAD + SB_HEAD_DIM + 3 + i] = -1.0
    return jnp.asarray(selq, BF16), jnp.asarray(onesq), jnp.asarray(selk, BF16), jnp.asarray(onesk)


def _proj_weights(w_in, q_norm, w_qb, kv_norm, w_kvb, fox_bias):
    depth = w_in.shape[0]
    cols = lambda a, n: w_in[:, :, a:a + n]
    t = lambda a: jnp.swapaxes(a, 1, 2)
    wt = jnp.concatenate([
        t(cols(_C_CQ, 384)), t(cols(_C_CKV, 256)), t(cols(_C_SQ, 256)), t(cols(_C_SV, 256)),
        t(cols(_C_FQ, 256)), t(cols(_C_FV, 256)), t(cols(_C_MQ, 256)), t(cols(_C_FF, 4)),
        jnp.zeros((depth, _T_END - _T_FF - HEADS, D_MODEL), F32)], axis=1).astype(BF16)
    kr = cols(_C_KR, MLA_ROPE)
    half = MLA_ROPE // 2
    rope = jnp.concatenate([kr, -kr[:, :, half:], kr[:, :, :half],
                            jnp.zeros((depth, D_MODEL, HEAD_PAD - 2 * MLA_ROPE), F32)], axis=2)
    wr = jnp.concatenate([
        rope, cols(_C_SK, 256), _placed(cols(_C_FK, 256), SB_HEAD_DIM),
        cols(_C_GATE, 1024), cols(_C_MR, MERGE_RANK)], axis=2).astype(BF16)
    kvb = w_kvb.reshape(depth, MLA_KV_RANK, HEADS, MLA_NOPE + HEAD_V)
    wkmla = jnp.pad(kvb[..., :MLA_NOPE], ((0, 0), (0, 0), (0, 0), (0, HEAD_PAD - MLA_NOPE)))
    wkmla = wkmla.reshape(depth, MLA_KV_RANK, HEADS * HEAD_PAD)
    wvt = t(kvb[..., MLA_NOPE:].reshape(depth, MLA_KV_RANK, HEADS * HEAD_V))
    fbias = jnp.pad(fox_bias, ((0, 0), (0, AUG - HEADS)))[:, :, None]
    return (wt, wr, q_norm[:, :, None], kv_norm[:, :, None],
            t(w_qb).astype(BF16), wvt.astype(BF16), wkmla.astype(BF16), fbias)


def _layer_spec(a, l):
    return pl.BlockSpec((None,) + a.shape[1:], lambda i, j: (l,) + (0,) * (a.ndim - 1))


def _proj(x, l, weights, consts, cos_t, sin_t, kmem, vmemt):
    b, s, d = x.shape
    lay = lambda a: _layer_spec(a, l)
    wt, wr, qn, kvn, wqbt, wvt, wkmla, fbias = weights
    selq, onesq, selk, onesk = consts
    full = lambda a: pl.BlockSpec(a.shape, lambda i, j: (0,) * a.ndim)
    qt_spec = pl.BlockSpec((1, HEADS * HEAD_PAD, TM), lambda i, j: (i, 0, j))
    k_spec = lambda w: pl.BlockSpec((1, TM, w), lambda i, j: (i, j, 0))
    vt_spec = lambda tk: pl.BlockSpec((1, TM // tk, HEADS * HEAD_V, tk), lambda i, j: (i, j, 0, 0))
    qt_shape = jax.ShapeDtypeStruct((b, HEADS * HEAD_PAD, s), BF16)
    k_shape = lambda w: jax.ShapeDtypeStruct((b, s, w), BF16)
    kw, kw_sb = HEADS * HEAD_PAD, HEADS * SB_HEAD_DIM
    vt_shape = lambda tk: jax.ShapeDtypeStruct((b, s // tk, HEADS * HEAD_V, tk), BF16)
    return pl.pallas_call(
        _proj_body,
        grid=(b, s // TM),
        in_specs=[pl.BlockSpec((1, TM, d), lambda i, j: (i, j, 0)),
                  lay(wt), lay(wr), lay(qn), lay(kvn), lay(wqbt), lay(wvt), lay(wkmla),
                  pl.BlockSpec((1, MLA_ROPE // 2, TM), lambda i, j: (i, 0, j)),
                  pl.BlockSpec((1, MLA_ROPE // 2, TM), lambda i, j: (i, 0, j)),
                  lay(fbias), full(selq), full(onesq), full(selk), full(onesk),
                  pl.BlockSpec((None, 1, HEADS, MEM_LEN, HEAD_V), lambda i, j: (l, i, 0, 0, 0)),
                  pl.BlockSpec((None, 1, HEADS, HEAD_V, MEM_LEN), lambda i, j: (l, i, 0, 0, 0))],
        out_specs=[qt_spec, k_spec(kw), vt_spec(TK_SM), qt_spec, k_spec(kw_sb), vt_spec(TK_SB), qt_spec, k_spec(kw),
                   vt_spec(TK_SM),
                   pl.BlockSpec((1, HEADS * HEAD_V, TM), lambda i, j: (i, 0, j)),
                   pl.BlockSpec((1, TM, N_BRANCHES * BRANCH_WIDTH), lambda i, j: (i, j, 0)),
                   pl.BlockSpec((1, TM, MERGE_RANK), lambda i, j: (i, j, 0)),
                   pl.BlockSpec((1, 1, 8, 128), lambda i, j: (i, j, 0, 0)),
                   pl.BlockSpec((1, TM // TK_SM, AUG, 128), lambda i, j: (i, j, 0, 0))],
        out_shape=[qt_shape, k_shape(kw), vt_shape(TK_SM), qt_shape, k_shape(kw_sb), vt_shape(TK_SB), qt_shape,
                   k_shape(kw), vt_shape(TK_SM),
                   jax.ShapeDtypeStruct((b, HEADS * HEAD_V, s), F32),
                   jax.ShapeDtypeStruct((b, s, N_BRANCHES * BRANCH_WIDTH), BF16),
                   jax.ShapeDtypeStruct((b, s, MERGE_RANK), BF16),
                   jax.ShapeDtypeStruct((b, s // TM, 8, 128), F32),
                   jax.ShapeDtypeStruct((b, s // TK_SM, AUG, 128), F32)],
        scratch_shapes=[pltpu.VMEM((AUG, 128), F32)],
        compiler_params=pltpu.CompilerParams(dimension_semantics=("arbitrary", "arbitrary"),
                                             vmem_limit_bytes=VMEM_LIMIT),
        name="fused_projection",
    )(x, wt, wr, qn, kvn, wqbt, wvt, wkmla, cos_t, sin_t, fbias, selq, onesq, selk, onesk, kmem, vmemt)


def _causal_masks(tq, tk, strict):
    row = lax.broadcasted_iota(jnp.int32, (tk, tq), 0)
    col = lax.broadcasted_iota(jnp.int32, (tk, tq), 1)
    return [(row + j * tk < col) if strict else (row + j * tk <= col) for j in range(tq // tk)]


def _softmax_attn_body(q_ref, k_ref, v_ref, *rest, tq, tk, hb, decay):
    if decay:
        kn_ref, cend_ref, o_ref, acc_ref, s_ref = rest
    else:
        o_ref, acc_ref, s_ref = rest
    nd = tq // tk
    qi = pl.program_id(2)
    n_full = qi * nd
    acc_ref[...] = jnp.zeros_like(acc_ref)

    def scores(k, h):
        rows = pl.ds(pl.multiple_of(k * tk, tk), tk)
        return _dot(k_ref[0, rows, h * HEAD_PAD:(h + 1) * HEAD_PAD], q_ref[0, h * HEAD_PAD:(h + 1) * HEAD_PAD, :])

    ones = jnp.ones((ACC_ROWS - HEAD_V, tk), BF16)

    def step(k, slot, carry, mask):
        k_next = jnp.maximum(k - 1, 0)
        out = [None] * hb

        def prefetch(h):
            if h < hb:
                s_ref[1 - slot, h] = scores(k_next, h)

        def softmax_pv(h):
            m = carry[h]
            s = s_ref[slot, h]
            if mask is not None:
                s = jnp.where(mask, s, NEG)
            m_new = jnp.maximum(m, jnp.max(s, axis=0, keepdims=True))
            p = jnp.exp2(s - m_new)
            if mask is not None:
                p = jnp.where(mask, p, 0.0)
            v1 = jnp.concatenate([v_ref[0, k, h * HEAD_V:(h + 1) * HEAD_V, :], ones], axis=0)
            ha = slice(h * ACC_ROWS, (h + 1) * ACC_ROWS)
            acc_ref[ha, :] = jnp.exp2(m - m_new) * acc_ref[ha, :] + _dot(v1, p.astype(BF16))
            out[h] = m_new

        prefetch(0)
        for h in range(hb):
            prefetch(h + 1)
            softmax_pv(h)
        return tuple(out)

    for h in range(hb):
        s_ref[(nd - 1) % 2, h] = scores(n_full + nd - 1, h)
    carry = tuple(jnp.full((1, tq), NEG, F32) for _ in range(hb))
    masks = _causal_masks(tq, tk, strict=False)
    for j in reversed(range(nd)):
        carry = step(n_full + j, j % 2, carry, masks[j])

    def pair(i, carry):
        k = n_full - 1 - 2 * i
        carry = step(k, 1, carry, None)
        return step(k - 1, 0, carry, None)

    if not decay:
        n_pairs = n_full // 2
        carry = lax.fori_loop(0, n_pairs // 2, lambda i, c: pair(2 * i + 1, pair(2 * i, c)), carry)
        carry = lax.cond(n_pairs % 2 == 1, lambda c: pair(n_pairs - 1, c), lambda c: c, carry)
    else:
        assert hb == HEADS
        kmax = jnp.sqrt(jnp.max(kn_ref[0], axis=0)[0:1, :]) * 1.01
        reach = []
        for h in range(hb):
            qh = q_ref[0, h * HEAD_PAD:h * HEAD_PAD + SB_HEAD_DIM + 8, :].astype(F32)
            qnorm = jnp.sqrt(jnp.sum(jnp.square(qh[0:SB_HEAD_DIM]), axis=0, keepdims=True))
            c_t = jnp.sum(qh[SB_HEAD_DIM:SB_HEAD_DIM + 3], axis=0, keepdims=True)
            reach.append(qnorm * kmax[:, 32 * h:32 * h + 1] + c_t)

        def alive(carry, k_low):
            c_end = cend_ref[0, jnp.maximum(k_low - 1, 0)]
            top = None
            for h in range(hb):
                gap = reach[h] - c_end[h:h + 1, 0:1] - carry[h]
                top = gap if top is None else jnp.maximum(top, gap)
            return jnp.max(top) > DEAD_LOG2

        def cond(state):
            i, live, _ = state
            return jnp.logical_and(i < n_full // 2, live)

        def body(state):
            i, _, carry = state
            carry = pair(i, carry)
            return i + 1, alive(carry, n_full - 2 * (i + 1)), carry

        carry = lax.while_loop(cond, body, (jnp.int32(0), alive(carry, n_full), carry))[2]
    for h in range(hb):
        num = acc_ref[h * ACC_ROWS:h * ACC_ROWS + HEAD_V, :]
        den = acc_ref[h * ACC_ROWS + HEAD_V:h * ACC_ROWS + HEAD_V + 1, :]
        o_ref[0, h * HEAD_V:(h + 1) * HEAD_V, :] = num * (1.0 / den)


def _sb_attn_body(q_ref, k_ref, v_ref, ntri_ref, o_ref, acc_ref, s_ref, *, tq, tk, hb):
    nd = tq // tk
    qi = pl.program_id(2)
    n_full = qi * nd
    acc_ref[...] = jnp.zeros_like(acc_ref)

    def scores(k, h):
        rows = pl.ds(pl.multiple_of(k * tk, tk), tk)
        pair_lanes = slice((h // 2) * HEAD_PAD, (h // 2 + 1) * HEAD_PAD)
        return _dot(k_ref[0, rows, pair_lanes], q_ref[0, h * HEAD_PAD:(h + 1) * HEAD_PAD, :])

    def step(k, slot, carry, mask):
        k_next = jnp.maximum(k - 1, 0)
        out = [None] * hb
        local = [None] * hb

        def suffix(h):
            if h >= hb:
                return
            z = s_ref[slot, h]
            neg_abs = pltpu.bitcast(pltpu.bitcast(z, jnp.uint32) | jnp.uint32(0x80000000), F32)
            sp = jnp.maximum(z, 0.0) + jnp.log(1.0 + jnp.exp2(neg_abs)) * LOG2E
            if mask is not None:
                sp = jnp.where(mask, sp, 0.0)
            local[h] = _dot(ntri_ref[...], sp.astype(BF16))
            s_ref[1 - slot, h] = scores(k_next, h)

        def weights_pv(h):
            cross = carry[h]
            a = jnp.exp2(s_ref[slot, h] + local[h])
            if mask is not None:
                a = jnp.where(mask, a, 0.0)
            hv = slice(h * HEAD_V, (h + 1) * HEAD_V)
            acc_ref[hv, :] += _dot(v_ref[0, k, hv, :], a.astype(BF16)) * jnp.exp2(cross)
            out[h] = cross + local[h][0:1, :]

        suffix(0)
        for h in range(hb):
            suffix(h + 1)
            weights_pv(h)
        return tuple(out)

    for h in range(hb):
        s_ref[(nd - 1) % 2, h] = scores(n_full + nd - 1, h)
    carry = tuple(jnp.zeros((1, tq), F32) for _ in range(hb))
    masks = _causal_masks(tq, tk, strict=True)
    for j in reversed(range(nd)):
        carry = step(n_full + j, j % 2, carry, masks[j])

    def pair(i, carry):
        k = n_full - 1 - 2 * i
        carry = step(k, 1, carry, None)
        return step(k - 1, 0, carry, None)

    def alive(carry):
        top = functools.reduce(jnp.maximum, carry)
        return jnp.max(top) > DEAD_LOG2

    def cond(state):
        i, live, _ = state
        return jnp.logical_and(i < n_full // 2, live)

    def body(state):
        i, _, carry = state
        carry = pair(i, carry)
        return i + 1, alive(carry), carry

    lax.while_loop(cond, body, (jnp.int32(0), alive(carry), carry))
    o_ref[0] = acc_ref[...]


def _attention(name, body, q_t, k, v_t, tq, tk, hb, acc_rows, extra=()):
    b, _, s = q_t.shape
    in_specs = [pl.BlockSpec((1, hb * HEAD_PAD, tq), lambda i, g, j: (i, g, j)),
                pl.BlockSpec((1, s, k.shape[2]), lambda i, g, j: (i, 0, g)),
                pl.BlockSpec((1, s // tk, hb * HEAD_V, tk), lambda i, g, j: (i, 0, g, 0))]
    in_specs += [spec for _, spec in extra]
    return pl.pallas_call(
        functools.partial(body, tq=tq, tk=tk, hb=hb),
        grid=(b, HEADS // hb, s // tq),
        in_specs=in_specs,
        out_specs=pl.BlockSpec((1, hb * HEAD_V, tq), lambda i, g, j: (i, g, j)),
        out_shape=jax.ShapeDtypeStruct((b, HEADS * HEAD_V, s), F32),
        scratch_shapes=[pltpu.VMEM((hb * acc_rows, tq), F32), pltpu.VMEM((2, hb, tk, tq), F32)],
        compiler_params=pltpu.CompilerParams(dimension_semantics=("arbitrary",) * 3,
                                             vmem_limit_bytes=VMEM_LIMIT),
        name=name,
    )(q_t, k, v_t, *[a for a, _ in extra])


def _out_body(x_ref, ymla_ref, ysb_ref, yfox_ref, ymem_ref, gate_ref, mr_ref, wmu_ref, wb_ref, wo_ref,
              g_ref, b_ref, o_ref):
    tm = x_ref.shape[1]
    parts = [slice(i * (tm // OUT_SPLIT), (i + 1) * (tm // OUT_SPLIT)) for i in range(OUT_SPLIT)]

    def merge(rows):
        mr = mr_ref[0, rows, :]
        merged = None
        for n, y_ref in enumerate((ymla_ref, ysb_ref, yfox_ref, ymem_ref)):
            y = y_ref[0, :, rows].T
            gz = gate_ref[0, rows, n * BRANCH_WIDTH:(n + 1) * BRANCH_WIDTH].astype(F32)
            yb = (y * (gz * _sigmoid(gz))).astype(BF16)
            branch = _dot(yb, wb_ref[n])
            mg = _sigmoid(_dot(mr, wmu_ref[:, n * D_MODEL:(n + 1) * D_MODEL]))
            merged = mg * branch if merged is None else merged + mg * branch
        return merged.astype(BF16)

    merged = [merge(rows) for rows in parts]
    outs = [_dot(m, wo_ref[...]) for m in merged]
    for rows, out in zip(parts, outs):
        r = DEEPNORM_ALPHA * x_ref[0, rows, :] + out
        mu = jnp.mean(r, axis=-1, keepdims=True)
        var = jnp.mean(jnp.square(r - mu), axis=-1, keepdims=True)
        o_ref[0, rows, :] = (r - mu) * lax.rsqrt(var + LN_EPS) * g_ref[...] + b_ref[...]


def _out(x, l, ys, gate, mr, wmu, wb, wo, g, bias):
    b, s, d = x.shape
    lay = lambda a: _layer_spec(a, l)
    yt_spec = pl.BlockSpec((1, BRANCH_WIDTH, TM), lambda i, j: (i, 0, j))
    return pl.pallas_call(
        _out_body,
        grid=(b, s // TM),
        in_specs=[pl.BlockSpec((1, TM, d), lambda i, j: (i, j, 0)), yt_spec, yt_spec, yt_spec, yt_spec,
                  pl.BlockSpec((1, TM, N_BRANCHES * BRANCH_WIDTH), lambda i, j: (i, j, 0)),
                  pl.BlockSpec((1, TM, MERGE_RANK), lambda i, j: (i, j, 0)),
                  lay(wmu), lay(wb), lay(wo), lay(g), lay(bias)],
        out_specs=pl.BlockSpec((1, TM, d), lambda i, j: (i, j, 0)),
        out_shape=jax.ShapeDtypeStruct((b, s, d), F32),
        compiler_params=pltpu.CompilerParams(dimension_semantics=("arbitrary", "arbitrary"),
                                             vmem_limit_bytes=VMEM_LIMIT),
        name="fused_output",
    )(x, *ys, gate, mr, wmu, wb, wo, g, bias)


def kernel(x, mem, positions, w_in, mla_q_norm, mla_w_qb, mla_kv_norm, mla_w_kvb, fox_forget_bias, w_mem_kv,
           w_merge_up, w_branch, w_out, ln_gain, ln_bias):
    b, s, d = x.shape
    assert d == D_MODEL and s % TM == 0 and s % TQ == 0
    assert all(tq % (2 * tk) == 0 and TM % tk == 0 and s % tq == 0 for tq, tk in ((TQ, TK_SM), (TQ_SB, TK_SB)))
    assert HB == HEADS
    depth = w_in.shape[0]

    cos_t, sin_t = _rope_tables(positions)

    mkv = _memkv(mem, w_mem_kv).reshape(depth, b, MEM_LEN, 2, HEADS, HEAD_V)
    kmem = mkv[:, :, :, 0].transpose(0, 1, 3, 2, 4).astype(BF16)
    vmemt = mkv[:, :, :, 1].transpose(0, 1, 3, 4, 2).astype(BF16)

    consts = _selection_constants()
    ntri = jnp.asarray(-np.triu(np.ones((TK_SB, TK_SB), np.float32)), BF16)
    weights = _proj_weights(w_in, mla_q_norm, mla_w_qb, mla_kv_norm, mla_w_kvb, fox_forget_bias)
    wmu, wb, wo = w_merge_up.astype(BF16), w_branch.astype(BF16), w_out.astype(BF16)
    ln_g, ln_b = ln_gain[:, None, :], ln_bias[:, None, :]
    whole = lambda a: pl.BlockSpec((1,) + a.shape[1:], lambda i, g, j: (i,) + (0,) * (a.ndim - 1))

    for l in range(depth):
        (qmla, kmla, vmla, qsb, ksb, vsb, qfox, kfox, vfox, ymem, gate, mr, kn, cend) = _proj(
            x, l, weights, consts, cos_t, sin_t, kmem, vmemt)
        ymla = _attention("mla_attention", functools.partial(_softmax_attn_body, decay=False), qmla, kmla, vmla,
                          TQ, TK_SM, HB, ACC_ROWS)
        ysb = _attention("stick_breaking_attention", _sb_attn_body, qsb, ksb, vsb, TQ_SB, TK_SB, HB, HEAD_V,
                         extra=((ntri, pl.BlockSpec(ntri.shape, lambda i, g, j: (0, 0))),))
        yfox = _attention("forgetting_attention", functools.partial(_softmax_attn_body, decay=True), qfox, kfox, vfox,
                          TQ, TK_SM, HB, ACC_ROWS, extra=((kn, whole(kn)), (cend, whole(cend))))
        x = _out(x, l, (ymla, ysb, yfox, ymem), gate, mr, wmu, wb, wo, ln_g, ln_b)
    return x
```

```python
import functools

import jax
import jax.numpy as jnp
import numpy as np
from jax import lax
from jax.experimental import pallas as pl
from jax.experimental.pallas import tpu as pltpu

F32, BF16 = jnp.float32, jnp.bfloat16

D_MODEL = 1024
DEPTH = 4
MEM_LEN = 256
N_BRANCHES = 4
BRANCH_WIDTH = 256
HEADS = 4
HEAD_V = 64
MLA_NOPE = 64
MLA_ROPE = 32
MLA_Q_RANK = 384
MLA_KV_RANK = 256
ROPE_THETA = 10000.0
SB_HEAD_DIM = 64
MERGE_RANK = 128
RMS_EPS = 1e-6
LN_EPS = 1e-5
DEEPNORM_ALPHA = (2 * DEPTH) ** 0.25

_C_CQ, _C_CKV, _C_KR = 0, 384, 640
_C_SQ, _C_SK, _C_SV = 672, 928, 1184
_C_FQ, _C_FK, _C_FV = 1440, 1696, 1952
_C_FF, _C_MQ, _C_GATE, _C_MR = 2208, 2212, 2468, 3492

HEAD_PAD = 128
TM = 1024
TQ = 512
TK_SM = 256
TQ_SB = 256
TK_SB = 128
HB = 4
OUT_SPLIT = 8
ACC_ROWS = HEAD_V + 16
LOG2E = 1.4426950408889634
DEAD_LOG2 = -160.0
NEG = -1e30
VMEM_LIMIT = 56 * 1024 * 1024

_T_CQ, _T_CKV, _T_SQ, _T_SV, _T_FQ, _T_FV, _T_MQ, _T_FF, _T_END = 0, 384, 640, 896, 1152, 1408, 1664, 1920, 1936
_R_ROPE, _R_SK, _R_FK, _R_GATE, _R_MR, _R_END = 0, 128, 384, 896, 1920, 2048
AUG = 16


def _dot(a, b):
    return jnp.dot(a, b, preferred_element_type=F32)


def _dot_nt(a, b):
    return lax.dot_general(a, b, (((1,), (1,)), ((), ())), preferred_element_type=F32)


def _sigmoid(v):
    return 0.5 * jnp.tanh(0.5 * v) + 0.5


def _split3(v):
    p1 = v.astype(BF16).astype(F32)
    r1 = v - p1
    p2 = r1.astype(BF16).astype(F32)
    p3 = (r1 - p2).astype(BF16).astype(F32)
    return p1, p2, p3


def _rope_body(pos_ref, invf_ref, cos_ref, sin_ref):
    ang = pos_ref[0].astype(F32) * invf_ref[...]
    cos_ref[0] = jnp.cos(ang)
    sin_ref[0] = jnp.sin(ang)


def _rope_tables(positions):
    b, s = positions.shape
    half = MLA_ROPE // 2
    inv_freq = ROPE_THETA ** (-jnp.arange(0, MLA_ROPE, 2, dtype=F32) / MLA_ROPE)
    return pl.pallas_call(
        _rope_body,
        grid=(b,),
        in_specs=[pl.BlockSpec((1, 1, s), lambda i: (i, 0, 0)),
                  pl.BlockSpec((half, 1), lambda i: (0, 0))],
        out_specs=[pl.BlockSpec((1, half, s), lambda i: (i, 0, 0))] * 2,
        out_shape=[jax.ShapeDtypeStruct((b, half, s), F32)] * 2,
        name="rope_tables",
    )(positions.reshape(b, 1, s), inv_freq.reshape(half, 1))


def _memkv_body(mem_ref, w_ref, o_ref):
    o_ref[0] = _dot(mem_ref[...].astype(BF16), w_ref[0].astype(BF16))


def _memkv(mem, w_mem_kv):
    b, m, d = mem.shape
    depth, _, n = w_mem_kv.shape
    return pl.pallas_call(
        _memkv_body,
        grid=(depth,),
        in_specs=[pl.BlockSpec((b * m, d), lambda l: (0, 0)),
                  pl.BlockSpec((1, d, n), lambda l: (l, 0, 0))],
        out_specs=pl.BlockSpec((1, b * m, n), lambda l: (l, 0, 0)),
        out_shape=jax.ShapeDtypeStruct((depth, b * m, n), F32),
        compiler_params=pltpu.CompilerParams(vmem_limit_bytes=VMEM_LIMIT),
        name="mem_kv_proj",
    )(mem.reshape(b * m, d), w_mem_kv)


def _proj_body(x_ref, wt_ref, wr_ref, qn_ref, kvn_ref, wqbt_ref, wvt_ref, wkmla_ref,
               cos_ref, sin_ref, fbias_ref, selq_ref, onesq_ref, selk_ref, onesk_ref,
               kmem_ref, vmemt_ref,
               qmla_o, kmla_o, vmla_o, qsb_o, ksb_o, vsb_o, qfox_o, kfox_o, vfox_o, ymem_o, gate_o, mr_o,
               kn_o, cend_o, carry_ref):
    tm = x_ref.shape[1]
    xb = x_ref[0].astype(BF16)

    def store_vt(o_ref, vt):
        tk = o_ref.shape[3]
        for j in range(tm // tk):
            o_ref[0, j] = vt[:, j * tk:(j + 1) * tk].astype(BF16)

    def rms(v, g):
        return v * lax.rsqrt(jnp.mean(v * v, axis=0, keepdims=True) + RMS_EPS) * g

    ff = _dot_nt(wt_ref[_T_FF:_T_END, :], xb) + fbias_ref[...]
    lf = jnp.minimum(ff, 0.0) - jnp.log1p(jnp.exp(-jnp.abs(ff)))
    upper = (lax.broadcasted_iota(jnp.int32, (tm, tm), 0) <= lax.broadcasted_iota(jnp.int32, (tm, tm), 1))
    cqn = rms(_dot_nt(wt_ref[_T_CQ:_T_CKV, :], xb), qn_ref[...]).astype(BF16)
    ckvn = rms(_dot_nt(wt_ref[_T_CKV:_T_SQ, :], xb), kvn_ref[...])
    ckvn_t, ckvn_r = ckvn.astype(BF16), ckvn.T.astype(BF16)
    cos_t, sin_t = cos_ref[0], sin_ref[0]
    tab = jnp.concatenate([cos_t, cos_t, sin_t, sin_t, jnp.zeros((HEAD_PAD - 2 * MLA_ROPE, tm), F32)], axis=0).T
    prod = _dot(xb, wr_ref[:, _R_ROPE:_R_SK]) * tab
    lane = lax.broadcasted_iota(jnp.int32, prod.shape, 1)
    kpe = jnp.where(lane < MLA_ROPE, prod + pltpu.roll(prod, 128 - MLA_ROPE, 1), 0.0)
    kpe = pltpu.roll(kpe, MLA_NOPE, 1)
    mq = (_dot_nt(wt_ref[_T_MQ:_T_FF, :], xb) * (HEAD_V ** -0.5)).astype(BF16)

    csum = _dot(jnp.concatenate(_split3(lf), axis=0).astype(BF16), upper.astype(BF16))

    @pl.when(pl.program_id(1) == 0)
    def _():
        carry_ref[...] = jnp.zeros_like(carry_ref)

    c = csum[0:AUG] + csum[AUG:2 * AUG] + csum[2 * AUG:3 * AUG] + carry_ref[:, 0:1]
    carry_ref[...] = jnp.broadcast_to(c[:, tm - 1:tm], carry_ref.shape)
    c_log2 = c * LOG2E
    cst = jnp.concatenate([*_split3(c_log2), jnp.zeros((HEAD_PAD - 3 * AUG, tm), F32)], axis=0)
    tk_fox = vfox_o.shape[3]
    for j in range(tm // tk_fox):
        cend_o[0, j] = jnp.broadcast_to(c_log2[:, (j + 1) * tk_fox - 1:(j + 1) * tk_fox], cend_o.shape[2:])

    sq = _dot_nt(wt_ref[_T_SQ:_T_SV, :], xb) * (SB_HEAD_DIM ** -0.5 * LOG2E)
    zs = jnp.zeros((HEAD_PAD - SB_HEAD_DIM, tm), F32)
    for h in range(HEADS):
        qh = sq[h * SB_HEAD_DIM:(h + 1) * SB_HEAD_DIM]
        blk = jnp.concatenate([qh, zs] if h % 2 == 0 else [zs, qh], axis=0)
        qsb_o[0, h * HEAD_PAD:(h + 1) * HEAD_PAD, :] = blk.astype(BF16)
    q = _dot(wqbt_ref[...], cqn) * ((MLA_NOPE + MLA_ROPE) ** -0.5 * LOG2E)
    half = MLA_ROPE // 2
    dq = MLA_NOPE + MLA_ROPE
    zq = jnp.zeros((HEAD_PAD - dq, tm), F32)
    for h in range(HEADS):
        nope = q[h * dq:h * dq + MLA_NOPE]
        x1 = q[h * dq + MLA_NOPE:h * dq + MLA_NOPE + half]
        x2 = q[h * dq + MLA_NOPE + half:(h + 1) * dq]
        blk = jnp.concatenate([nope, x1 * cos_t - x2 * sin_t, x1 * sin_t + x2 * cos_t, zq], axis=0)
        qmla_o[0, h * HEAD_PAD:(h + 1) * HEAD_PAD, :] = blk.astype(BF16)

    store_vt(vsb_o, _dot_nt(wt_ref[_T_SV:_T_FQ, :], xb))
    store_vt(vmla_o, _dot(wvt_ref[...], ckvn_t))
    ksb_o[0] = _dot(xb, wr_ref[:, _R_SK:_R_FK]).astype(BF16)
    kmla_o[0] = (_dot(ckvn_r, wkmla_ref[...]) + jnp.concatenate([kpe] * HEADS, axis=1)).astype(BF16)

    fq = _dot_nt(wt_ref[_T_FQ:_T_FV, :], xb) * (SB_HEAD_DIM ** -0.5 * LOG2E)
    aug_t = _dot(selq_ref[...], cst.astype(BF16)) + onesq_ref[...]
    zf = jnp.zeros((HEAD_PAD - SB_HEAD_DIM - 8, tm), F32)
    for h in range(HEADS):
        blk = jnp.concatenate([fq[h * SB_HEAD_DIM:(h + 1) * SB_HEAD_DIM], aug_t[h * 8:(h + 1) * 8], zf], axis=0)
        qfox_o[0, h * HEAD_PAD:(h + 1) * HEAD_PAD, :] = blk.astype(BF16)
    store_vt(vfox_o, _dot_nt(wt_ref[_T_FV:_T_MQ, :], xb))
    kfox = _dot(xb, wr_ref[:, _R_FK:_R_GATE]) + _dot(cst.T.astype(BF16), selk_ref[...]) + onesk_ref[...]
    kfox = kfox.astype(BF16)
    kfox_o[0] = kfox
    ksq = jnp.square(kfox.astype(F32))
    group = lax.broadcasted_iota(jnp.int32, kn_o.shape[2:], 1) // 32
    kn = jnp.zeros(kn_o.shape[2:], F32)
    for h in range(HEADS):
        n2 = jnp.sum(ksq[:, h * HEAD_PAD:h * HEAD_PAD + SB_HEAD_DIM], axis=1, keepdims=True)
        kn = jnp.where(group == h, jnp.max(n2, axis=0, keepdims=True), kn)
    kn_o[0, 0] = kn

    s_mem = [_dot(kmem_ref[0, h], mq[h * HEAD_V:(h + 1) * HEAD_V]) for h in range(HEADS)]
    gate_o[0] = _dot(xb, wr_ref[:, _R_GATE:_R_MR]).astype(BF16)
    mr_o[0] = _dot(xb, wr_ref[:, _R_MR:_R_END]).astype(BF16)
    for h in range(HEADS):
        p = jnp.exp(s_mem[h] - jnp.max(s_mem[h], axis=0, keepdims=True))
        o = _dot(vmemt_ref[0, h], p.astype(BF16))
        ymem_o[0, h * HEAD_V:(h + 1) * HEAD_V, :] = o / jnp.sum(p, axis=0, keepdims=True)


def _placed(w, n_in):
    depth, d, _ = w.shape
    w = w.reshape(depth, d, HEADS, n_in)
    return jnp.pad(w, ((0, 0), (0, 0), (0, 0), (0, HEAD_PAD - n_in))).reshape(depth, d, HEADS * HEAD_PAD)


def _selection_constants():
    selq = np.zeros((HEADS * 8, HEAD_PAD), np.float32)
    onesq = np.zeros((HEADS * 8, 1), np.float32)
    selk = np.zeros((HEAD_PAD, HEADS * HEAD_PAD), np.float32)
    onesk = np.zeros((1, HEADS * HEAD_PAD), np.float32)
    for h in range(HEADS):
        for i in range(3):
            selq[h * 8 + i, i * AUG + h] = 1.0
            onesq[h * 8 + 3 + i, 0] = 1.0
            onesk[0, h * HEAD_PAD + SB_HEAD_DIM + i] = 1.0
            selk[i * AUG + h, h * HEAD_PAD + SB_HEAD_DIM + 3 + i] = -1.0
    return jnp.asarray(selq, BF16), jnp.asarray(onesq), jnp.asarray(selk, BF16), jnp.asarray(onesk)


def _proj_weights(w_in, q_norm, w_qb, kv_norm, w_kvb, fox_bias):
    depth = w_in.shape[0]
    cols = lambda a, n: w_in[:, :, a:a + n]
    t = lambda a: jnp.swapaxes(a, 1, 2)
    wt = jnp.concatenate([
        t(cols(_C_CQ, 384)), t(cols(_C_CKV, 256)), t(cols(_C_SQ, 256)), t(cols(_C_SV, 256)),
        t(cols(_C_FQ, 256)), t(cols(_C_FV, 256)), t(cols(_C_MQ, 256)), t(cols(_C_FF, 4)),
        jnp.zeros((depth, _T_END - _T_FF - HEADS, D_MODEL), F32)], axis=1).astype(BF16)
    kr = cols(_C_KR, MLA_ROPE)
    half = MLA_ROPE // 2
    rope = jnp.concatenate([kr, -kr[:, :, half:], kr[:, :, :half],
                            jnp.zeros((depth, D_MODEL, HEAD_PAD - 2 * MLA_ROPE), F32)], axis=2)
    wr = jnp.concatenate([
        rope, cols(_C_SK, 256), _placed(cols(_C_FK, 256), SB_HEAD_DIM),
        cols(_C_GATE, 1024), cols(_C_MR, MERGE_RANK)], axis=2).astype(BF16)
    kvb = w_kvb.reshape(depth, MLA_KV_RANK, HEADS, MLA_NOPE + HEAD_V)
    wkmla = jnp.pad(kvb[..., :MLA_NOPE], ((0, 0), (0, 0), (0, 0), (0, HEAD_PAD - MLA_NOPE)))
    wkmla = wkmla.reshape(depth, MLA_KV_RANK, HEADS * HEAD_PAD)
    wvt = t(kvb[..., MLA_NOPE:].reshape(depth, MLA_KV_RANK, HEADS * HEAD_V))
    fbias = jnp.pad(fox_bias, ((0, 0), (0, AUG - HEADS)))[:, :, None]
    return (wt, wr, q_norm[:, :, None], kv_norm[:, :, None],
            t(w_qb).astype(BF16), wvt.astype(BF16), wkmla.astype(BF16), fbias)


def _layer_spec(a, l):
    return pl.BlockSpec((None,) + a.shape[1:], lambda i, j: (l,) + (0,) * (a.ndim - 1))


def _proj(x, l, weights, consts, cos_t, sin_t, kmem, vmemt):
    b, s, d = x.shape
    lay = lambda a: _layer_spec(a, l)
    wt, wr, qn, kvn, wqbt, wvt, wkmla, fbias = weights
    selq, onesq, selk, onesk = consts
    full = lambda a: pl.BlockSpec(a.shape, lambda i, j: (0,) * a.ndim)
    qt_spec = pl.BlockSpec((1, HEADS * HEAD_PAD, TM), lambda i, j: (i, 0, j))
    k_spec = lambda w: pl.BlockSpec((1, TM, w), lambda i, j: (i, j, 0))
    vt_spec = lambda tk: pl.BlockSpec((1, TM // tk, HEADS * HEAD_V, tk), lambda i, j: (i, j, 0, 0))
    qt_shape = jax.ShapeDtypeStruct((b, HEADS * HEAD_PAD, s), BF16)
    k_shape = lambda w: jax.ShapeDtypeStruct((b, s, w), BF16)
    kw, kw_sb = HEADS * HEAD_PAD, HEADS * SB_HEAD_DIM
    vt_shape = lambda tk: jax.ShapeDtypeStruct((b, s // tk, HEADS * HEAD_V, tk), BF16)
    return pl.pallas_call(
        _proj_body,
        grid=(b, s // TM),
        in_specs=[pl.BlockSpec((1, TM, d), lambda i, j: (i, j, 0)),
                  lay(wt), lay(wr), lay(qn), lay(kvn), lay(wqbt), lay(wvt), lay(wkmla),
                  pl.BlockSpec((1, MLA_ROPE // 2, TM), lambda i, j: (i, 0, j)),
                  pl.BlockSpec((1, MLA_ROPE // 2, TM), lambda i, j: (i, 0, j)),
                  lay(fbias), full(selq), full(onesq), full(selk), full(onesk),
                  pl.BlockSpec((None, 1, HEADS, MEM_LEN, HEAD_V), lambda i, j: (l, i, 0, 0, 0)),
                  pl.BlockSpec((None, 1, HEADS, HEAD_V, MEM_LEN), lambda i, j: (l, i, 0, 0, 0))],
        out_specs=[qt_spec, k_spec(kw), vt_spec(TK_SM), qt_spec, k_spec(kw_sb), vt_spec(TK_SB), qt_spec, k_spec(kw),
                   vt_spec(TK_SM),
                   pl.BlockSpec((1, HEADS * HEAD_V, TM), lambda i, j: (i, 0, j)),
                   pl.BlockSpec((1, TM, N_BRANCHES * BRANCH_WIDTH), lambda i, j: (i, j, 0)),
                   pl.BlockSpec((1, TM, MERGE_RANK), lambda i, j: (i, j, 0)),
                   pl.BlockSpec((1, 1, 8, 128), lambda i, j: (i, j, 0, 0)),
                   pl.BlockSpec((1, TM // TK_SM, AUG, 128), lambda i, j: (i, j, 0, 0))],
        out_shape=[qt_shape, k_shape(kw), vt_shape(TK_SM), qt_shape, k_shape(kw_sb), vt_shape(TK_SB), qt_shape,
                   k_shape(kw), vt_shape(TK_SM),
                   jax.ShapeDtypeStruct((b, HEADS * HEAD_V, s), F32),
                   jax.ShapeDtypeStruct((b, s, N_BRANCHES * BRANCH_WIDTH), BF16),
                   jax.ShapeDtypeStruct((b, s, MERGE_RANK), BF16),
                   jax.ShapeDtypeStruct((b, s // TM, 8, 128), F32),
                   jax.ShapeDtypeStruct((b, s // TK_SM, AUG, 128), F32)],
        scratch_shapes=[pltpu.VMEM((AUG, 128), F32)],
        compiler_params=pltpu.CompilerParams(dimension_semantics=("arbitrary", "arbitrary"),
                                             vmem_limit_bytes=VMEM_LIMIT),
        name="fused_projection",
    )(x, wt, wr, qn, kvn, wqbt, wvt, wkmla, cos_t, sin_t, fbias, selq, onesq, selk, onesk, kmem, vmemt)


def _causal_masks(tq, tk, strict):
    row = lax.broadcasted_iota(jnp.int32, (tk, tq), 0)
    col = lax.broadcasted_iota(jnp.int32, (tk, tq), 1)
    return [(row + j * tk < col) if strict else (row + j * tk <= col) for j in range(tq // tk)]


def _softmax_attn_body(q_ref, k_ref, v_ref, *rest, tq, tk, hb, decay):
    if decay:
        kn_ref, cend_ref, o_ref, acc_ref, s_ref = rest
    else:
        o_ref, acc_ref, s_ref = rest
    nd = tq // tk
    qi = pl.program_id(2)
    n_full = qi * nd
    acc_ref[...] = jnp.zeros_like(acc_ref)

    def scores(k, h, lo):
        rows = pl.ds(pl.multiple_of(k * tk, tk), tk)
        return _dot(k_ref[0, rows, h * HEAD_PAD:(h + 1) * HEAD_PAD], q_ref[0, h * HEAD_PAD:(h + 1) * HEAD_PAD, lo:])

    ones = jnp.ones((ACC_ROWS - HEAD_V, tk), BF16)

    def step(k, slot, carry, mask=None, lo=0, lo_next=0):
        k_next = jnp.maximum(k - 1, 0)
        out = [None] * hb

        def prefetch(h):
            if h < hb:
                s_ref[1 - slot, h, :, lo_next:] = scores(k_next, h, lo_next)

        def softmax_pv(h):
            m = carry[h][:, lo:]
            s = s_ref[slot, h, :, lo:]
            if mask is not None:
                s = jnp.where(mask[:, lo:], s, NEG)
            m_new = jnp.maximum(m, jnp.max(s, axis=0, keepdims=True))
            p = jnp.exp2(s - m_new)
            if mask is not None:
                p = jnp.where(mask[:, lo:], p, 0.0)
            v1 = jnp.concatenate([v_ref[0, k, h * HEAD_V:(h + 1) * HEAD_V, :], ones], axis=0)
            ha = slice(h * ACC_ROWS, (h + 1) * ACC_ROWS)
            acc_ref[ha, lo:] = jnp.exp2(m - m_new) * acc_ref[ha, lo:] + _dot(v1, p.astype(BF16))
            out[h] = m_new if lo == 0 else jnp.concatenate([carry[h][:, :lo], m_new], axis=1)

        prefetch(0)
        for h in range(hb):
            prefetch(h + 1)
            softmax_pv(h)
        return tuple(out)

    for h in range(hb):
        s_ref[(nd - 1) % 2, h, :, (nd - 1) * tk:] = scores(n_full + nd - 1, h, (nd - 1) * tk)
    carry = tuple(jnp.full((1, tq), NEG, F32) for _ in range(hb))
    masks = _causal_masks(tq, tk, strict=False)
    for j in reversed(range(nd)):
        carry = step(n_full + j, j % 2, carry, masks[j], lo=j * tk, lo_next=max(j - 1, 0) * tk)

    def pair(i, carry):
        k = n_full - 1 - 2 * i
        return step(k - 1, 0, step(k, 1, carry))

    if not decay:
        n_pairs = n_full // 2
        carry = lax.fori_loop(0, n_pairs // 2, lambda i, c: pair(2 * i + 1, pair(2 * i, c)), carry)
        carry = lax.cond(n_pairs % 2 == 1, lambda c: pair(n_pairs - 1, c), lambda c: c, carry)
    else:
        assert hb == HEADS
        kmax = jnp.sqrt(jnp.max(kn_ref[0], axis=0)[0:1, :]) * 1.01
        reach = []
        for h in range(hb):
            qh = q_ref[0, h * HEAD_PAD:h * HEAD_PAD + SB_HEAD_DIM + 8, :].astype(F32)
            qnorm = jnp.sqrt(jnp.sum(jnp.square(qh[0:SB_HEAD_DIM]), axis=0, keepdims=True))
            c_t = jnp.sum(qh[SB_HEAD_DIM:SB_HEAD_DIM + 3], axis=0, keepdims=True)
            reach.append(qnorm * kmax[:, 32 * h:32 * h + 1] + c_t)

        def alive(carry, k_low):
            c_end = cend_ref[0, jnp.maximum(k_low - 1, 0)]
            top = None
            for h in range(hb):
                gap = reach[h] - c_end[h:h + 1, 0:1] - carry[h]
                top = gap if top is None else jnp.maximum(top, gap)
            return jnp.max(top) > DEAD_LOG2

        def cond(state):
            i, live, _ = state
            return jnp.logical_and(i < n_full // 2, live)

        def body(state):
            i, _, carry = state
            carry = pair(i, carry)
            return i + 1, alive(carry, n_full - 2 * (i + 1)), carry

        carry = lax.while_loop(cond, body, (jnp.int32(0), alive(carry, n_full), carry))[2]
    for h in range(hb):
        num = acc_ref[h * ACC_ROWS:h * ACC_ROWS + HEAD_V, :]
        den = acc_ref[h * ACC_ROWS + HEAD_V:h * ACC_ROWS + HEAD_V + 1, :]
        o_ref[0, h * HEAD_V:(h + 1) * HEAD_V, :] = num * (1.0 / den)


def _sb_attn_body(q_ref, k_ref, v_ref, ntri_ref, o_ref, acc_ref, s_ref, *, tq, tk, hb):
    nd = tq // tk
    qi = pl.program_id(2)
    n_full = qi * nd
    acc_ref[...] = jnp.zeros_like(acc_ref)

    def scores(k, h, lo):
        rows = pl.ds(pl.multiple_of(k * tk, tk), tk)
        pair_lanes = slice((h // 2) * HEAD_PAD, (h // 2 + 1) * HEAD_PAD)
        return _dot(k_ref[0, rows, pair_lanes], q_ref[0, h * HEAD_PAD:(h + 1) * HEAD_PAD, lo:])

    def step(k, slot, carry, mask=None, lo=0, lo_next=0):
        k_next = jnp.maximum(k - 1, 0)
        out = [None] * hb
        local = [None] * hb

        def suffix(h):
            if h >= hb:
                return
            z = s_ref[slot, h, :, lo:]
            neg_abs = pltpu.bitcast(pltpu.bitcast(z, jnp.uint32) | jnp.uint32(0x80000000), F32)
            sp = jnp.maximum(z, 0.0) + jnp.log(1.0 + jnp.exp2(neg_abs)) * LOG2E
            if mask is not None:
                sp = jnp.where(mask[:, lo:], sp, 0.0)
            local[h] = _dot(ntri_ref[...], sp.astype(BF16))
            s_ref[1 - slot, h, :, lo_next:] = scores(k_next, h, lo_next)

        def weights_pv(h):
            cross = carry[h][:, lo:]
            a = jnp.exp2(s_ref[slot, h, :, lo:] + local[h])
            if mask is not None:
                a = jnp.where(mask[:, lo:], a, 0.0)
            hv = slice(h * HEAD_V, (h + 1) * HEAD_V)
            acc_ref[hv, lo:] += _dot(v_ref[0, k, hv, :], a.astype(BF16)) * jnp.exp2(cross)
            cross = cross + local[h][0:1, :]
            out[h] = cross if lo == 0 else jnp.concatenate([carry[h][:, :lo], cross], axis=1)

        suffix(0)
        for h in range(hb):
            suffix(h + 1)
            weights_pv(h)
        return tuple(out)

    for h in range(hb):
        s_ref[(nd - 1) % 2, h, :, (nd - 1) * tk:] = scores(n_full + nd - 1, h, (nd - 1) * tk)
    carry = tuple(jnp.zeros((1, tq), F32) for _ in range(hb))
    masks = _causal_masks(tq, tk, strict=True)
    for j in reversed(range(nd)):
        carry = step(n_full + j, j % 2, carry, masks[j], lo=j * tk, lo_next=max(j - 1, 0) * tk)

    def pair(i, carry):
        k = n_full - 1 - 2 * i
        return step(k - 1, 0, step(k, 1, carry))

    def alive(carry):
        top = functools.reduce(jnp.maximum, carry)
        return jnp.max(top) > DEAD_LOG2

    def cond(state):
        i, live, _ = state
        return jnp.logical_and(i < n_full // 2, live)

    def body(state):
        i, _, carry = state
        carry = pair(i, carry)
        return i + 1, alive(carry), carry

    lax.while_loop(cond, body, (jnp.int32(0), alive(carry), carry))
    o_ref[0] = acc_ref[...]


def _attention(name, body, q_t, k, v_t, tq, tk, hb, acc_rows, extra=()):
    b, _, s = q_t.shape
    in_specs = [pl.BlockSpec((1, hb * HEAD_PAD, tq), lambda i, g, j: (i, g, j)),
                pl.BlockSpec((1, s, k.shape[2]), lambda i, g, j: (i, 0, g)),
                pl.BlockSpec((1, s // tk, hb * HEAD_V, tk), lambda i, g, j: (i, 0, g, 0))]
    in_specs += [spec for _, spec in extra]
    return pl.pallas_call(
        functools.partial(body, tq=tq, tk=tk, hb=hb),
        grid=(b, HEADS // hb, s // tq),
        in_specs=in_specs,
        out_specs=pl.BlockSpec((1, hb * HEAD_V, tq), lambda i, g, j: (i, g, j)),
        out_shape=jax.ShapeDtypeStruct((b, HEADS * HEAD_V, s), F32),
        scratch_shapes=[pltpu.VMEM((hb * acc_rows, tq), F32), pltpu.VMEM((2, hb, tk, tq), F32)],
        compiler_params=pltpu.CompilerParams(dimension_semantics=("arbitrary",) * 3,
                                             vmem_limit_bytes=VMEM_LIMIT),
        name=name,
    )(q_t, k, v_t, *[a for a, _ in extra])


def _out_body(x_ref, ymla_ref, ysb_ref, yfox_ref, ymem_ref, gate_ref, mr_ref, wmu_ref, wb_ref, wo_ref,
              g_ref, b_ref, o_ref):
    tm = x_ref.shape[1]
    parts = [slice(i * (tm // OUT_SPLIT), (i + 1) * (tm // OUT_SPLIT)) for i in range(OUT_SPLIT)]

    def merge(rows):
        mr = mr_ref[0, rows, :]
        merged = None
        for n, y_ref in enumerate((ymla_ref, ysb_ref, yfox_ref, ymem_ref)):
            y = y_ref[0, :, rows].T
            gz = gate_ref[0, rows, n * BRANCH_WIDTH:(n + 1) * BRANCH_WIDTH].astype(F32)
            yb = (y * (gz * _sigmoid(gz))).astype(BF16)
            branch = _dot(yb, wb_ref[n])
            mg = _sigmoid(_dot(mr, wmu_ref[:, n * D_MODEL:(n + 1) * D_MODEL]))
            merged = mg * branch if merged is None else merged + mg * branch
        return merged.astype(BF16)

    merged = [merge(rows) for rows in parts]
    outs = [_dot(m, wo_ref[...]) for m in merged]
    for rows, out in zip(parts, outs):
        r = DEEPNORM_ALPHA * x_ref[0, rows, :] + out
        mu = jnp.mean(r, axis=-1, keepdims=True)
        var = jnp.mean(jnp.square(r - mu), axis=-1, keepdims=True)
        o_ref[0, rows, :] = (r - mu) * lax.rsqrt(var + LN_EPS) * g_ref[...] + b_ref[...]


def _out(x, l, ys, gate, mr, wmu, wb, wo, g, bias):
    b, s, d = x.shape
    lay = lambda a: _layer_spec(a, l)
    yt_spec = pl.BlockSpec((1, BRANCH_WIDTH, TM), lambda i, j: (i, 0, j))
    return pl.pallas_call(
        _out_body,
        grid=(b, s // TM),
        in_specs=[pl.BlockSpec((1, TM, d), lambda i, j: (i, j, 0)), yt_spec, yt_spec, yt_spec, yt_spec,
                  pl.BlockSpec((1, TM, N_BRANCHES * BRANCH_WIDTH), lambda i, j: (i, j, 0)),
                  pl.BlockSpec((1, TM, MERGE_RANK), lambda i, j: (i, j, 0)),
                  lay(wmu), lay(wb), lay(wo), lay(g), lay(bias)],
        out_specs=pl.BlockSpec((1, TM, d), lambda i, j: (i, j, 0)),
        out_shape=jax.ShapeDtypeStruct((b, s, d), F32),
        compiler_params=pltpu.CompilerParams(dimension_semantics=("arbitrary", "arbitrary"),
                                             vmem_limit_bytes=VMEM_LIMIT),
        name="fused_output",
    )(x, *ys, gate, mr, wmu, wb, wo, g, bias)


def kernel(x, mem, positions, w_in, mla_q_norm, mla_w_qb, mla_kv_norm, mla_w_kvb, fox_forget_bias, w_mem_kv,
           w_merge_up, w_branch, w_out, ln_gain, ln_bias):
    b, s, d = x.shape
    assert d == D_MODEL and s % TM == 0 and s % TQ == 0
    assert all(tq % (2 * tk) == 0 and TM % tk == 0 and s % tq == 0 for tq, tk in ((TQ, TK_SM), (TQ_SB, TK_SB)))
    assert HB == HEADS
    depth = w_in.shape[0]

    cos_t, sin_t = _rope_tables(positions)

    mkv = _memkv(mem, w_mem_kv).reshape(depth, b, MEM_LEN, 2, HEADS, HEAD_V)
    kmem = mkv[:, :, :, 0].transpose(0, 1, 3, 2, 4).astype(BF16)
    vmemt = mkv[:, :, :, 1].transpose(0, 1, 3, 4, 2).astype(BF16)

    consts = _selection_constants()
    ntri = jnp.asarray(-np.triu(np.ones((TK_SB, TK_SB), np.float32)), BF16)
    weights = _proj_weights(w_in, mla_q_norm, mla_w_qb, mla_kv_norm, mla_w_kvb, fox_forget_bias)
    wmu, wb, wo = w_merge_up.astype(BF16), w_branch.astype(BF16), w_out.astype(BF16)
    ln_g, ln_b = ln_gain[:, None, :], ln_bias[:, None, :]
    whole = lambda a: pl.BlockSpec((1,) + a.shape[1:], lambda i, g, j: (i,) + (0,) * (a.ndim - 1))

    for l in range(depth):
        (qmla, kmla, vmla, qsb, ksb, vsb, qfox, kfox, vfox, ymem, gate, mr, kn, cend) = _proj(
            x, l, weights, consts, cos_t, sin_t, kmem, vmemt)
        ymla = _attention("mla_attention", functools.partial(_softmax_attn_body, decay=False), qmla, kmla, vmla,
                          TQ, TK_SM, HB, ACC_ROWS)
        ysb = _attention("stick_breaking_attention", _sb_attn_body, qsb, ksb, vsb, TQ_SB, TK_SB, HB, HEAD_V,
                         extra=((ntri, pl.BlockSpec(ntri.shape, lambda i, g, j: (0, 0))),))
        yfox = _attention("forgetting_attention", functools.partial(_softmax_attn_body, decay=True), qfox, kfox, vfox,
                          TQ, TK_SM, HB, ACC_ROWS, extra=((kn, whole(kn)), (cend, whole(cend))))
        x = _out(x, l, (ymla, ysb, yfox, ymem), gate, mr, wmu, wb, wo, ln_g, ln_b)
    return x
```

```python
import functools

import jax
import jax.numpy as jnp
import numpy as np
from jax import lax
from jax.experimental import pallas as pl
from jax.experimental.pallas import tpu as pltpu

F32, BF16 = jnp.float32, jnp.bfloat16

D_MODEL = 1024
DEPTH = 4
MEM_LEN = 256
N_BRANCHES = 4
BRANCH_WIDTH = 256
HEADS = 4
HEAD_V = 64
MLA_NOPE = 64
MLA_ROPE = 32
MLA_Q_RANK = 384
MLA_KV_RANK = 256
ROPE_THETA = 10000.0
SB_HEAD_DIM = 64
MERGE_RANK = 128
RMS_EPS = 1e-6
LN_EPS = 1e-5
DEEPNORM_ALPHA = (2 * DEPTH) ** 0.25

_C_CQ, _C_CKV, _C_KR = 0, 384, 640
_C_SQ, _C_SK, _C_SV = 672, 928, 1184
_C_FQ, _C_FK, _C_FV = 1440, 1696, 1952
_C_FF, _C_MQ, _C_GATE, _C_MR = 2208, 2212, 2468, 3492

HEAD_PAD = 128
TM = 1024
TQ = 512
TK_SM = 256
TQ_SB = 256
TK_SB = 128
HB = 4
OUT_SPLIT = 8
ACC_ROWS = HEAD_V + 16
LOG2E = 1.4426950408889634
DEAD_LOG2 = -160.0
NEG = -1e30
LANES, SUBLANES = 128, 8
V7X_VMEM_BYTES = 64 * 1024 * 1024
VMEM_LIMIT = V7X_VMEM_BYTES * 7 // 8

_T_CQ, _T_CKV, _T_SQ, _T_SV, _T_FQ, _T_FV, _T_MQ, _T_FF, _T_END = 0, 384, 640, 896, 1152, 1408, 1664, 1920, 1936
_R_ROPE, _R_SK, _R_FK, _R_GATE, _R_MR, _R_END = 0, 128, 384, 896, 1920, 2048
AUG = 16


def _dot(a, b):
    return jnp.dot(a, b, preferred_element_type=F32)


def _dot_nt(a, b):
    return lax.dot_general(a, b, (((1,), (1,)), ((), ())), preferred_element_type=F32)


def _sigmoid(v):
    return 0.5 * jnp.tanh(0.5 * v) + 0.5


def _split3(v):
    p1 = v.astype(BF16).astype(F32)
    r1 = v - p1
    p2 = r1.astype(BF16).astype(F32)
    p3 = (r1 - p2).astype(BF16).astype(F32)
    return p1, p2, p3


def _rope_body(pos_ref, invf_ref, cos_ref, sin_ref):
    ang = pos_ref[0].astype(F32) * invf_ref[...]
    cos_ref[0] = jnp.cos(ang)
    sin_ref[0] = jnp.sin(ang)


def _rope_tables(positions):
    b, s = positions.shape
    half = MLA_ROPE // 2
    inv_freq = ROPE_THETA ** (-jnp.arange(0, MLA_ROPE, 2, dtype=F32) / MLA_ROPE)
    return pl.pallas_call(
        _rope_body,
        grid=(b,),
        in_specs=[pl.BlockSpec((1, 1, s), lambda i: (i, 0, 0)),
                  pl.BlockSpec((half, 1), lambda i: (0, 0))],
        out_specs=[pl.BlockSpec((1, half, s), lambda i: (i, 0, 0))] * 2,
        out_shape=[jax.ShapeDtypeStruct((b, half, s), F32)] * 2,
        name="rope_tables",
    )(positions.reshape(b, 1, s), inv_freq.reshape(half, 1))


def _memkv_body(mem_ref, w_ref, o_ref):
    o_ref[0] = _dot(mem_ref[...].astype(BF16), w_ref[0].astype(BF16))


def _memkv(mem, w_mem_kv):
    b, m, d = mem.shape
    depth, _, n = w_mem_kv.shape
    return pl.pallas_call(
        _memkv_body,
        grid=(depth,),
        in_specs=[pl.BlockSpec((b * m, d), lambda l: (0, 0)),
                  pl.BlockSpec((1, d, n), lambda l: (l, 0, 0))],
        out_specs=pl.BlockSpec((1, b * m, n), lambda l: (l, 0, 0)),
        out_shape=jax.ShapeDtypeStruct((depth, b * m, n), F32),
        compiler_params=pltpu.CompilerParams(vmem_limit_bytes=VMEM_LIMIT),
        name="mem_kv_proj",
    )(mem.reshape(b * m, d), w_mem_kv)


def _proj_body(x_ref, wt_ref, wr_ref, qn_ref, kvn_ref, wqbt_ref, wvt_ref, wkmla_ref,
               cos_ref, sin_ref, fbias_ref, selq_ref, onesq_ref, selk_ref, onesk_ref,
               kmem_ref, vmemt_ref,
               qmla_o, kmla_o, vmla_o, qsb_o, ksb_o, vsb_o, qfox_o, kfox_o, vfox_o, ymem_o, gate_o, mr_o,
               kn_o, cend_o, carry_ref):
    tm = x_ref.shape[1]
    xb = x_ref[0].astype(BF16)

    def store_vt(o_ref, vt):
        tk = o_ref.shape[3]
        for j in range(tm // tk):
            o_ref[0, j] = vt[:, j * tk:(j + 1) * tk].astype(BF16)

    def rms(v, g):
        return v * lax.rsqrt(jnp.mean(v * v, axis=0, keepdims=True) + RMS_EPS) * g

    ff = _dot_nt(wt_ref[_T_FF:_T_END, :], xb) + fbias_ref[...]
    lf = jnp.minimum(ff, 0.0) - jnp.log1p(jnp.exp(-jnp.abs(ff)))
    upper = (lax.broadcasted_iota(jnp.int32, (tm, tm), 0) <= lax.broadcasted_iota(jnp.int32, (tm, tm), 1))
    cqn = rms(_dot_nt(wt_ref[_T_CQ:_T_CKV, :], xb), qn_ref[...]).astype(BF16)
    ckvn = rms(_dot_nt(wt_ref[_T_CKV:_T_SQ, :], xb), kvn_ref[...])
    ckvn_t, ckvn_r = ckvn.astype(BF16), ckvn.T.astype(BF16)
    cos_t, sin_t = cos_ref[0], sin_ref[0]
    tab = jnp.concatenate([cos_t, cos_t, sin_t, sin_t, jnp.zeros((HEAD_PAD - 2 * MLA_ROPE, tm), F32)], axis=0).T
    prod = _dot(xb, wr_ref[:, _R_ROPE:_R_SK]) * tab
    lane = lax.broadcasted_iota(jnp.int32, prod.shape, 1)
    kpe = jnp.where(lane < MLA_ROPE, prod + pltpu.roll(prod, HEAD_PAD - MLA_ROPE, 1), 0.0)
    kpe = pltpu.roll(kpe, MLA_NOPE, 1)
    mq = (_dot_nt(wt_ref[_T_MQ:_T_FF, :], xb) * (HEAD_V ** -0.5)).astype(BF16)

    csum = _dot(jnp.concatenate(_split3(lf), axis=0).astype(BF16), upper.astype(BF16))

    @pl.when(pl.program_id(1) == 0)
    def _():
        carry_ref[...] = jnp.zeros_like(carry_ref)

    c = csum[0:AUG] + csum[AUG:2 * AUG] + csum[2 * AUG:3 * AUG] + carry_ref[:, 0:1]
    carry_ref[...] = jnp.broadcast_to(c[:, tm - 1:tm], carry_ref.shape)
    c_log2 = c * LOG2E
    cst = jnp.concatenate([*_split3(c_log2), jnp.zeros((HEAD_PAD - 3 * AUG, tm), F32)], axis=0)
    tk_fox = vfox_o.shape[3]
    for j in range(tm // tk_fox):
        cend_o[0, j] = jnp.broadcast_to(c_log2[:, (j + 1) * tk_fox - 1:(j + 1) * tk_fox], cend_o.shape[2:])

    sq = _dot_nt(wt_ref[_T_SQ:_T_SV, :], xb) * (SB_HEAD_DIM ** -0.5 * LOG2E)
    zs = jnp.zeros((HEAD_PAD - SB_HEAD_DIM, tm), F32)
    for h in range(HEADS):
        qh = sq[h * SB_HEAD_DIM:(h + 1) * SB_HEAD_DIM]
        blk = jnp.concatenate([qh, zs] if h % 2 == 0 else [zs, qh], axis=0)
        qsb_o[0, h * HEAD_PAD:(h + 1) * HEAD_PAD, :] = blk.astype(BF16)
    q = _dot(wqbt_ref[...], cqn) * ((MLA_NOPE + MLA_ROPE) ** -0.5 * LOG2E)
    half = MLA_ROPE // 2
    dq = MLA_NOPE + MLA_ROPE
    zq = jnp.zeros((HEAD_PAD - dq, tm), F32)
    for h in range(HEADS):
        nope = q[h * dq:h * dq + MLA_NOPE]
        x1 = q[h * dq + MLA_NOPE:h * dq + MLA_NOPE + half]
        x2 = q[h * dq + MLA_NOPE + half:(h + 1) * dq]
        blk = jnp.concatenate([nope, x1 * cos_t - x2 * sin_t, x1 * sin_t + x2 * cos_t, zq], axis=0)
        qmla_o[0, h * HEAD_PAD:(h + 1) * HEAD_PAD, :] = blk.astype(BF16)

    store_vt(vsb_o, _dot_nt(wt_ref[_T_SV:_T_FQ, :], xb))
    store_vt(vmla_o, _dot(wvt_ref[...], ckvn_t))
    ksb_o[0] = _dot(xb, wr_ref[:, _R_SK:_R_FK]).astype(BF16)
    kmla_o[0] = (_dot(ckvn_r, wkmla_ref[...]) + jnp.concatenate([kpe] * HEADS, axis=1)).astype(BF16)

    fq = _dot_nt(wt_ref[_T_FQ:_T_FV, :], xb) * (SB_HEAD_DIM ** -0.5 * LOG2E)
    aug_t = _dot(selq_ref[...], cst.astype(BF16)) + onesq_ref[...]
    zf = jnp.zeros((HEAD_PAD - SB_HEAD_DIM - 8, tm), F32)
    for h in range(HEADS):
        blk = jnp.concatenate([fq[h * SB_HEAD_DIM:(h + 1) * SB_HEAD_DIM], aug_t[h * 8:(h + 1) * 8], zf], axis=0)
        qfox_o[0, h * HEAD_PAD:(h + 1) * HEAD_PAD, :] = blk.astype(BF16)
    store_vt(vfox_o, _dot_nt(wt_ref[_T_FV:_T_MQ, :], xb))
    kfox = _dot(xb, wr_ref[:, _R_FK:_R_GATE]) + _dot(cst.T.astype(BF16), selk_ref[...]) + onesk_ref[...]
    kfox = kfox.astype(BF16)
    kfox_o[0] = kfox
    ksq = jnp.square(kfox.astype(F32))
    group = lax.broadcasted_iota(jnp.int32, kn_o.shape[2:], 1) // 32
    kn = jnp.zeros(kn_o.shape[2:], F32)
    for h in range(HEADS):
        n2 = jnp.sum(ksq[:, h * HEAD_PAD:h * HEAD_PAD + SB_HEAD_DIM], axis=1, keepdims=True)
        kn = jnp.where(group == h, jnp.max(n2, axis=0, keepdims=True), kn)
    kn_o[0, 0] = kn

    s_mem = [_dot(kmem_ref[0, h], mq[h * HEAD_V:(h + 1) * HEAD_V]) for h in range(HEADS)]
    gate_o[0] = _dot(xb, wr_ref[:, _R_GATE:_R_MR]).astype(BF16)
    mr_o[0] = _dot(xb, wr_ref[:, _R_MR:_R_END]).astype(BF16)
    for h in range(HEADS):
        p = jnp.exp(s_mem[h] - jnp.max(s_mem[h], axis=0, keepdims=True))
        o = _dot(vmemt_ref[0, h], p.astype(BF16))
        ymem_o[0, h * HEAD_V:(h + 1) * HEAD_V, :] = o / jnp.sum(p, axis=0, keepdims=True)


def _placed(w, n_in):
    depth, d, _ = w.shape
    w = w.reshape(depth, d, HEADS, n_in)
    return jnp.pad(w, ((0, 0), (0, 0), (0, 0), (0, HEAD_PAD - n_in))).reshape(depth, d, HEADS * HEAD_PAD)


def _selection_constants():
    selq = np.zeros((HEADS * 8, HEAD_PAD), np.float32)
    onesq = np.zeros((HEADS * 8, 1), np.float32)
    selk = np.zeros((HEAD_PAD, HEADS * HEAD_PAD), np.float32)
    onesk = np.zeros((1, HEADS * HEAD_PAD), np.float32)
    for h in range(HEADS):
        for i in range(3):
            selq[h * 8 + i, i * AUG + h] = 1.0
            onesq[h * 8 + 3 + i, 0] = 1.0
            onesk[0, h * HEAD_PAD + SB_HEAD_DIM + i] = 1.0
            selk[i * AUG + h, h * HEAD_PAD + SB_HEAD_DIM + 3 + i] = -1.0
    return jnp.asarray(selq, BF16), jnp.asarray(onesq), jnp.asarray(selk, BF16), jnp.asarray(onesk)


def _proj_weights(w_in, q_norm, w_qb, kv_norm, w_kvb, fox_bias):
    depth = w_in.shape[0]
    cols = lambda a, n: w_in[:, :, a:a + n]
    t = lambda a: jnp.swapaxes(a, 1, 2)
    wt = jnp.concatenate([
        t(cols(_C_CQ, MLA_Q_RANK)), t(cols(_C_CKV, MLA_KV_RANK)), t(cols(_C_SQ, BRANCH_WIDTH)),
        t(cols(_C_SV, BRANCH_WIDTH)), t(cols(_C_FQ, BRANCH_WIDTH)), t(cols(_C_FV, BRANCH_WIDTH)),
        t(cols(_C_MQ, BRANCH_WIDTH)), t(cols(_C_FF, HEADS)),
        jnp.zeros((depth, _T_END - _T_FF - HEADS, D_MODEL), F32)], axis=1).astype(BF16)
    kr = cols(_C_KR, MLA_ROPE)
    half = MLA_ROPE // 2
    rope = jnp.concatenate([kr, -kr[:, :, half:], kr[:, :, :half],
                            jnp.zeros((depth, D_MODEL, HEAD_PAD - 2 * MLA_ROPE), F32)], axis=2)
    wr = jnp.concatenate([
        rope, cols(_C_SK, BRANCH_WIDTH), _placed(cols(_C_FK, BRANCH_WIDTH), SB_HEAD_DIM),
        cols(_C_GATE, N_BRANCHES * BRANCH_WIDTH), cols(_C_MR, MERGE_RANK)], axis=2).astype(BF16)
    kvb = w_kvb.reshape(depth, MLA_KV_RANK, HEADS, MLA_NOPE + HEAD_V)
    wkmla = jnp.pad(kvb[..., :MLA_NOPE], ((0, 0), (0, 0), (0, 0), (0, HEAD_PAD - MLA_NOPE)))
    wkmla = wkmla.reshape(depth, MLA_KV_RANK, HEADS * HEAD_PAD)
    wvt = t(kvb[..., MLA_NOPE:].reshape(depth, MLA_KV_RANK, HEADS * HEAD_V))
    fbias = jnp.pad(fox_bias, ((0, 0), (0, AUG - HEADS)))[:, :, None]
    return (wt, wr, q_norm[:, :, None], kv_norm[:, :, None],
            t(w_qb).astype(BF16), wvt.astype(BF16), wkmla.astype(BF16), fbias)


def _layer_spec(a, l):
    return pl.BlockSpec((None,) + a.shape[1:], lambda i, j: (l,) + (0,) * (a.ndim - 1))


def _proj(x, l, weights, consts, cos_t, sin_t, kmem, vmemt):
    b, s, d = x.shape
    lay = lambda a: _layer_spec(a, l)
    wt, wr, qn, kvn, wqbt, wvt, wkmla, fbias = weights
    selq, onesq, selk, onesk = consts
    full = lambda a: pl.BlockSpec(a.shape, lambda i, j: (0,) * a.ndim)
    qt_spec = pl.BlockSpec((1, HEADS * HEAD_PAD, TM), lambda i, j: (i, 0, j))
    k_spec = lambda w: pl.BlockSpec((1, TM, w), lambda i, j: (i, j, 0))
    vt_spec = lambda tk: pl.BlockSpec((1, TM // tk, HEADS * HEAD_V, tk), lambda i, j: (i, j, 0, 0))
    qt_shape = jax.ShapeDtypeStruct((b, HEADS * HEAD_PAD, s), BF16)
    k_shape = lambda w: jax.ShapeDtypeStruct((b, s, w), BF16)
    kw, kw_sb = HEADS * HEAD_PAD, HEADS * SB_HEAD_DIM
    vt_shape = lambda tk: jax.ShapeDtypeStruct((b, s // tk, HEADS * HEAD_V, tk), BF16)
    return pl.pallas_call(
        _proj_body,
        grid=(b, s // TM),
        in_specs=[pl.BlockSpec((1, TM, d), lambda i, j: (i, j, 0)),
                  lay(wt), lay(wr), lay(qn), lay(kvn), lay(wqbt), lay(wvt), lay(wkmla),
                  pl.BlockSpec((1, MLA_ROPE // 2, TM), lambda i, j: (i, 0, j)),
                  pl.BlockSpec((1, MLA_ROPE // 2, TM), lambda i, j: (i, 0, j)),
                  lay(fbias), full(selq), full(onesq), full(selk), full(onesk),
                  pl.BlockSpec((None, 1, HEADS, MEM_LEN, HEAD_V), lambda i, j: (l, i, 0, 0, 0)),
                  pl.BlockSpec((None, 1, HEADS, HEAD_V, MEM_LEN), lambda i, j: (l, i, 0, 0, 0))],
        out_specs=[qt_spec, k_spec(kw), vt_spec(TK_SM), qt_spec, k_spec(kw_sb), vt_spec(TK_SB), qt_spec, k_spec(kw),
                   vt_spec(TK_SM),
                   pl.BlockSpec((1, HEADS * HEAD_V, TM), lambda i, j: (i, 0, j)),
                   pl.BlockSpec((1, TM, N_BRANCHES * BRANCH_WIDTH), lambda i, j: (i, j, 0)),
                   pl.BlockSpec((1, TM, MERGE_RANK), lambda i, j: (i, j, 0)),
                   pl.BlockSpec((1, 1, SUBLANES, LANES), lambda i, j: (i, j, 0, 0)),
                   pl.BlockSpec((1, TM // TK_SM, AUG, LANES), lambda i, j: (i, j, 0, 0))],
        out_shape=[qt_shape, k_shape(kw), vt_shape(TK_SM), qt_shape, k_shape(kw_sb), vt_shape(TK_SB), qt_shape,
                   k_shape(kw), vt_shape(TK_SM),
                   jax.ShapeDtypeStruct((b, HEADS * HEAD_V, s), F32),
                   jax.ShapeDtypeStruct((b, s, N_BRANCHES * BRANCH_WIDTH), BF16),
                   jax.ShapeDtypeStruct((b, s, MERGE_RANK), BF16),
                   jax.ShapeDtypeStruct((b, s // TM, SUBLANES, LANES), F32),
                   jax.ShapeDtypeStruct((b, s // TK_SM, AUG, LANES), F32)],
        scratch_shapes=[pltpu.VMEM((AUG, LANES), F32)],
        compiler_params=pltpu.CompilerParams(dimension_semantics=("arbitrary", "arbitrary"),
                                             vmem_limit_bytes=VMEM_LIMIT),
        name="fused_projection",
    )(x, wt, wr, qn, kvn, wqbt, wvt, wkmla, cos_t, sin_t, fbias, selq, onesq, selk, onesk, kmem, vmemt)


def _causal_masks(tq, tk, strict):
    row = lax.broadcasted_iota(jnp.int32, (tk, tq), 0)
    col = lax.broadcasted_iota(jnp.int32, (tk, tq), 1)
    return [(row + j * tk < col) if strict else (row + j * tk <= col) for j in range(tq // tk)]


def _softmax_attn_body(q_ref, k_ref, v_ref, *rest, tq, tk, hb, decay):
    if decay:
        kn_ref, cend_ref, o_ref, acc_ref, s_ref = rest
    else:
        o_ref, acc_ref, s_ref = rest
    nd = tq // tk
    qi = pl.program_id(2)
    n_full = qi * nd
    acc_ref[...] = jnp.zeros_like(acc_ref)

    def scores(k, h, lo):
        rows = pl.ds(pl.multiple_of(k * tk, tk), tk)
        return _dot(k_ref[0, rows, h * HEAD_PAD:(h + 1) * HEAD_PAD], q_ref[0, h * HEAD_PAD:(h + 1) * HEAD_PAD, lo:])

    ones = jnp.ones((ACC_ROWS - HEAD_V, tk), BF16)

    def step(k, slot, carry, mask=None, lo=0, lo_next=0):
        k_next = jnp.maximum(k - 1, 0)
        out = [None] * hb

        def prefetch(h):
            if h < hb:
                s_ref[1 - slot, h, :, lo_next:] = scores(k_next, h, lo_next)

        def softmax_pv(h):
            m = carry[h][:, lo:]
            s = s_ref[slot, h, :, lo:]
            if mask is not None:
                s = jnp.where(mask[:, lo:], s, NEG)
            m_new = jnp.maximum(m, jnp.max(s, axis=0, keepdims=True))
            p = jnp.exp2(s - m_new)
            if mask is not None:
                p = jnp.where(mask[:, lo:], p, 0.0)
            v1 = jnp.concatenate([v_ref[0, k, h * HEAD_V:(h + 1) * HEAD_V, :], ones], axis=0)
            ha = slice(h * ACC_ROWS, (h + 1) * ACC_ROWS)
            acc_ref[ha, lo:] = jnp.exp2(m - m_new) * acc_ref[ha, lo:] + _dot(v1, p.astype(BF16))
            out[h] = m_new if lo == 0 else jnp.concatenate([carry[h][:, :lo], m_new], axis=1)

        prefetch(0)
        for h in range(hb):
            prefetch(h + 1)
            softmax_pv(h)
        return tuple(out)

    for h in range(hb):
        s_ref[(nd - 1) % 2, h, :, (nd - 1) * tk:] = scores(n_full + nd - 1, h, (nd - 1) * tk)
    carry = tuple(jnp.full((1, tq), NEG, F32) for _ in range(hb))
    masks = _causal_masks(tq, tk, strict=False)
    for j in reversed(range(nd)):
        carry = step(n_full + j, j % 2, carry, masks[j], lo=j * tk, lo_next=max(j - 1, 0) * tk)

    def pair(i, carry):
        k = n_full - 1 - 2 * i
        return step(k - 1, 0, step(k, 1, carry))

    if not decay:
        n_pairs = n_full // 2
        quad = lambda i, c: pair(4 * i + 3, pair(4 * i + 2, pair(4 * i + 1, pair(4 * i, c))))
        carry = lax.fori_loop(0, n_pairs // 4, quad, carry)
        carry = lax.fori_loop(n_pairs - n_pairs % 4, n_pairs, pair, carry)
    else:
        assert hb == HEADS
        kmax = jnp.sqrt(jnp.max(kn_ref[0], axis=0)[0:1, :]) * 1.01
        reach = []
        for h in range(hb):
            qh = q_ref[0, h * HEAD_PAD:h * HEAD_PAD + SB_HEAD_DIM + 8, :].astype(F32)
            qnorm = jnp.sqrt(jnp.sum(jnp.square(qh[0:SB_HEAD_DIM]), axis=0, keepdims=True))
            c_t = jnp.sum(qh[SB_HEAD_DIM:SB_HEAD_DIM + 3], axis=0, keepdims=True)
            reach.append(qnorm * kmax[:, 32 * h:32 * h + 1] + c_t)

        def alive(carry, k_low):
            c_end = cend_ref[0, jnp.maximum(k_low - 1, 0)]
            top = None
            for h in range(hb):
                gap = reach[h] - c_end[h:h + 1, 0:1] - carry[h]
                top = gap if top is None else jnp.maximum(top, gap)
            return jnp.max(top) > DEAD_LOG2

        def cond(state):
            i, live, _ = state
            return jnp.logical_and(i < n_full // 2, live)

        def body(state):
            i, _, carry = state
            carry = pair(i, carry)
            return i + 1, alive(carry, n_full - 2 * (i + 1)), carry

        carry = lax.while_loop(cond, body, (jnp.int32(0), alive(carry, n_full), carry))[2]
    for h in range(hb):
        num = acc_ref[h * ACC_ROWS:h * ACC_ROWS + HEAD_V, :]
        den = acc_ref[h * ACC_ROWS + HEAD_V:h * ACC_ROWS + HEAD_V + 1, :]
        o_ref[0, h * HEAD_V:(h + 1) * HEAD_V, :] = num * (1.0 / den)


def _sb_attn_body(q_ref, k_ref, v_ref, ntri_ref, o_ref, acc_ref, s_ref, *, tq, tk, hb):
    nd = tq // tk
    qi = pl.program_id(2)
    n_full = qi * nd
    acc_ref[...] = jnp.zeros_like(acc_ref)

    def scores(k, h, lo):
        rows = pl.ds(pl.multiple_of(k * tk, tk), tk)
        pair_lanes = slice((h // 2) * HEAD_PAD, (h // 2 + 1) * HEAD_PAD)
        return _dot(k_ref[0, rows, pair_lanes], q_ref[0, h * HEAD_PAD:(h + 1) * HEAD_PAD, lo:])

    def step(k, slot, carry, mask=None, lo=0, lo_next=0):
        k_next = jnp.maximum(k - 1, 0)
        out = [None] * hb
        local = [None] * hb

        def suffix(h):
            if h >= hb:
                return
            z = s_ref[slot, h, :, lo:]
            neg_abs = pltpu.bitcast(pltpu.bitcast(z, jnp.uint32) | jnp.uint32(0x80000000), F32)
            sp = jnp.maximum(z, 0.0) + jnp.log(1.0 + jnp.exp2(neg_abs)) * LOG2E
            if mask is not None:
                sp = jnp.where(mask[:, lo:], sp, 0.0)
            local[h] = _dot(ntri_ref[...], sp.astype(BF16))
            s_ref[1 - slot, h, :, lo_next:] = scores(k_next, h, lo_next)

        def weights_pv(h):
            cross = carry[h][:, lo:]
            a = jnp.exp2(s_ref[slot, h, :, lo:] + local[h])
            if mask is not None:
                a = jnp.where(mask[:, lo:], a, 0.0)
            hv = slice(h * HEAD_V, (h + 1) * HEAD_V)
            acc_ref[hv, lo:] += _dot(v_ref[0, k, hv, :], a.astype(BF16)) * jnp.exp2(cross)
            cross = cross + local[h][0:1, :]
            out[h] = cross if lo == 0 else jnp.concatenate([carry[h][:, :lo], cross], axis=1)

        suffix(0)
        for h in range(hb):
            suffix(h + 1)
            weights_pv(h)
        return tuple(out)

    for h in range(hb):
        s_ref[(nd - 1) % 2, h, :, (nd - 1) * tk:] = scores(n_full + nd - 1, h, (nd - 1) * tk)
    carry = tuple(jnp.zeros((1, tq), F32) for _ in range(hb))
    masks = _causal_masks(tq, tk, strict=True)
    for j in reversed(range(nd)):
        carry = step(n_full + j, j % 2, carry, masks[j], lo=j * tk, lo_next=max(j - 1, 0) * tk)

    def pair(i, carry):
        k = n_full - 1 - 2 * i
        return step(k - 1, 0, step(k, 1, carry))

    def alive(carry):
        top = functools.reduce(jnp.maximum, carry)
        return jnp.max(top) > DEAD_LOG2

    def cond(state):
        i, live, _ = state
        return jnp.logical_and(i < n_full // 2, live)

    def body(state):
        i, _, carry = state
        carry = pair(i, carry)
        return i + 1, alive(carry), carry

    lax.while_loop(cond, body, (jnp.int32(0), alive(carry), carry))
    o_ref[0] = acc_ref[...]


def _attention(name, body, q_t, k, v_t, tq, tk, hb, acc_rows, extra=()):
    b, _, s = q_t.shape
    in_specs = [pl.BlockSpec((1, hb * HEAD_PAD, tq), lambda i, g, j: (i, g, j)),
                pl.BlockSpec((1, s, k.shape[2]), lambda i, g, j: (i, 0, g)),
                pl.BlockSpec((1, s // tk, hb * HEAD_V, tk), lambda i, g, j: (i, 0, g, 0))]
    in_specs += [spec for _, spec in extra]
    return pl.pallas_call(
        functools.partial(body, tq=tq, tk=tk, hb=hb),
        grid=(b, HEADS // hb, s // tq),
        in_specs=in_specs,
        out_specs=pl.BlockSpec((1, hb * HEAD_V, tq), lambda i, g, j: (i, g, j)),
        out_shape=jax.ShapeDtypeStruct((b, HEADS * HEAD_V, s), F32),
        scratch_shapes=[pltpu.VMEM((hb * acc_rows, tq), F32), pltpu.VMEM((2, hb, tk, tq), F32)],
        compiler_params=pltpu.CompilerParams(dimension_semantics=("arbitrary",) * 3,
                                             vmem_limit_bytes=VMEM_LIMIT),
        name=name,
    )(q_t, k, v_t, *[a for a, _ in extra])


def _out_body(x_ref, ymla_ref, ysb_ref, yfox_ref, ymem_ref, gate_ref, mr_ref, wmu_ref, wb_ref, wo_ref,
              g_ref, b_ref, o_ref):
    tm = x_ref.shape[1]
    parts = [slice(i * (tm // OUT_SPLIT), (i + 1) * (tm // OUT_SPLIT)) for i in range(OUT_SPLIT)]

    def merge(rows):
        mr = mr_ref[0, rows, :]
        merged = None
        for n, y_ref in enumerate((ymla_ref, ysb_ref, yfox_ref, ymem_ref)):
            y = y_ref[0, :, rows].T
            gz = gate_ref[0, rows, n * BRANCH_WIDTH:(n + 1) * BRANCH_WIDTH].astype(F32)
            yb = (y * (gz * _sigmoid(gz))).astype(BF16)
            branch = _dot(yb, wb_ref[n])
            mg = _sigmoid(_dot(mr, wmu_ref[:, n * D_MODEL:(n + 1) * D_MODEL]))
            merged = mg * branch if merged is None else merged + mg * branch
        return merged.astype(BF16)

    merged = [merge(rows) for rows in parts]
    outs = [_dot(m, wo_ref[...]) for m in merged]
    for rows, out in zip(parts, outs):
        r = DEEPNORM_ALPHA * x_ref[0, rows, :] + out
        mu = jnp.mean(r, axis=-1, keepdims=True)
        var = jnp.mean(jnp.square(r - mu), axis=-1, keepdims=True)
        o_ref[0, rows, :] = (r - mu) * lax.rsqrt(var + LN_EPS) * g_ref[...] + b_ref[...]


def _out(x, l, ys, gate, mr, wmu, wb, wo, g, bias):
    b, s, d = x.shape
    lay = lambda a: _layer_spec(a, l)
    yt_spec = pl.BlockSpec((1, BRANCH_WIDTH, TM), lambda i, j: (i, 0, j))
    return pl.pallas_call(
        _out_body,
        grid=(b, s // TM),
        in_specs=[pl.BlockSpec((1, TM, d), lambda i, j: (i, j, 0)), yt_spec, yt_spec, yt_spec, yt_spec,
                  pl.BlockSpec((1, TM, N_BRANCHES * BRANCH_WIDTH), lambda i, j: (i, j, 0)),
                  pl.BlockSpec((1, TM, MERGE_RANK), lambda i, j: (i, j, 0)),
                  lay(wmu), lay(wb), lay(wo), lay(g), lay(bias)],
        out_specs=pl.BlockSpec((1, TM, d), lambda i, j: (i, j, 0)),
        out_shape=jax.ShapeDtypeStruct((b, s, d), F32),
        compiler_params=pltpu.CompilerParams(dimension_semantics=("arbitrary", "arbitrary"),
                                             vmem_limit_bytes=VMEM_LIMIT),
        name="fused_output",
    )(x, *ys, gate, mr, wmu, wb, wo, g, bias)


def kernel(x, mem, positions, w_in, mla_q_norm, mla_w_qb, mla_kv_norm, mla_w_kvb, fox_forget_bias, w_mem_kv,
           w_merge_up, w_branch, w_out, ln_gain, ln_bias):
    b, s, d = x.shape
    assert d == D_MODEL and s % TM == 0 and s % TQ == 0
    assert all(tq % (2 * tk) == 0 and TM % tk == 0 and s % tq == 0 for tq, tk in ((TQ, TK_SM), (TQ_SB, TK_SB)))
    assert HB == HEADS
    depth = w_in.shape[0]

    cos_t, sin_t = _rope_tables(positions)

    mkv = _memkv(mem, w_mem_kv).reshape(depth, b, MEM_LEN, 2, HEADS, HEAD_V)
    kmem = mkv[:, :, :, 0].transpose(0, 1, 3, 2, 4).astype(BF16)
    vmemt = mkv[:, :, :, 1].transpose(0, 1, 3, 4, 2).astype(BF16)

    consts = _selection_constants()
    ntri = jnp.asarray(-np.triu(np.ones((TK_SB, TK_SB), np.float32)), BF16)
    weights = _proj_weights(w_in, mla_q_norm, mla_w_qb, mla_kv_norm, mla_w_kvb, fox_forget_bias)
    wmu, wb, wo = w_merge_up.astype(BF16), w_branch.astype(BF16), w_out.astype(BF16)
    ln_g, ln_b = ln_gain[:, None, :], ln_bias[:, None, :]
    whole = lambda a: pl.BlockSpec((1,) + a.shape[1:], lambda i, g, j: (i,) + (0,) * (a.ndim - 1))

    for l in range(depth):
        (qmla, kmla, vmla, qsb, ksb, vsb, qfox, kfox, vfox, ymem, gate, mr, kn, cend) = _proj(
            x, l, weights, consts, cos_t, sin_t, kmem, vmemt)
        ymla = _attention("mla_attention", functools.partial(_softmax_attn_body, decay=False), qmla, kmla, vmla,
                          TQ, TK_SM, HB, ACC_ROWS)
        ysb = _attention("stick_breaking_attention", _sb_attn_body, qsb, ksb, vsb, TQ_SB, TK_SB, HB, HEAD_V,
                         extra=((ntri, pl.BlockSpec(ntri.shape, lambda i, g, j: (0, 0))),))
        yfox = _attention("forgetting_attention", functools.partial(_softmax_attn_body, decay=True), qfox, kfox, vfox,
                          TQ, TK_SM, HB, ACC_ROWS, extra=((kn, whole(kn)), (cend, whole(cend))))
        x = _out(x, l, (ymla, ysb, yfox, ymem), gate, mr, wmu, wb, wo, ln_g, ln_b)
    return x
```

```python
import functools

import jax
import jax.numpy as jnp
import numpy as np
from jax import lax
from jax.experimental import pallas as pl
from jax.experimental.pallas import tpu as pltpu

F32, BF16 = jnp.float32, jnp.bfloat16

D_MODEL = 1024
DEPTH = 4
MEM_LEN = 256
N_BRANCHES = 4
BRANCH_WIDTH = 256
HEADS = 4
HEAD_V = 64
MLA_NOPE = 64
MLA_ROPE = 32
MLA_Q_RANK = 384
MLA_KV_RANK = 256
ROPE_THETA = 10000.0
SB_HEAD_DIM = 64
MERGE_RANK = 128
RMS_EPS = 1e-6
LN_EPS = 1e-5
DEEPNORM_ALPHA = (2 * DEPTH) ** 0.25

_C_CQ, _C_CKV, _C_KR = 0, 384, 640
_C_SQ, _C_SK, _C_SV = 672, 928, 1184
_C_FQ, _C_FK, _C_FV = 1440, 1696, 1952
_C_FF, _C_MQ, _C_GATE, _C_MR = 2208, 2212, 2468, 3492

HEAD_PAD = 128
TM = 1024
TQ = 512
TK_SM = 256
TQ_SB = 256
TK_SB = 128
HB = 4
OUT_SPLIT = 8
ACC_ROWS = HEAD_V + 16
LOG2E = 1.4426950408889634
DEAD_LOG2 = -160.0
NEG = -1e30
LANES, SUBLANES = 128, 8
V7X_VMEM_BYTES = 64 * 1024 * 1024
VMEM_LIMIT = V7X_VMEM_BYTES * 7 // 8

_T_CQ, _T_CKV, _T_SQ, _T_SV, _T_FQ, _T_FV, _T_MQ, _T_FF, _T_END = 0, 384, 640, 896, 1152, 1408, 1664, 1920, 1936
_R_ROPE, _R_SK, _R_FK, _R_GATE, _R_MR, _R_END = 0, 128, 384, 896, 1920, 2048
AUG = 16


def _dot(a, b):
    return jnp.dot(a, b, preferred_element_type=F32)


def _dot_nt(a, b):
    return lax.dot_general(a, b, (((1,), (1,)), ((), ())), preferred_element_type=F32)


def _sigmoid(v):
    return 0.5 * jnp.tanh(0.5 * v) + 0.5


def _split3(v):
    p1 = v.astype(BF16).astype(F32)
    r1 = v - p1
    p2 = r1.astype(BF16).astype(F32)
    p3 = (r1 - p2).astype(BF16).astype(F32)
    return p1, p2, p3


def _rope_body(pos_ref, invf_ref, cos_ref, sin_ref):
    ang = pos_ref[0].astype(F32) * invf_ref[...]
    cos_ref[0] = jnp.cos(ang)
    sin_ref[0] = jnp.sin(ang)


def _rope_tables(positions):
    b, s = positions.shape
    half = MLA_ROPE // 2
    inv_freq = ROPE_THETA ** (-jnp.arange(0, MLA_ROPE, 2, dtype=F32) / MLA_ROPE)
    return pl.pallas_call(
        _rope_body,
        grid=(b,),
        in_specs=[pl.BlockSpec((1, 1, s), lambda i: (i, 0, 0)),
                  pl.BlockSpec((half, 1), lambda i: (0, 0))],
        out_specs=[pl.BlockSpec((1, half, s), lambda i: (i, 0, 0))] * 2,
        out_shape=[jax.ShapeDtypeStruct((b, half, s), F32)] * 2,
        name="rope_tables",
    )(positions.reshape(b, 1, s), inv_freq.reshape(half, 1))


def _memkv_body(mem_ref, w_ref, o_ref):
    o_ref[0] = _dot(mem_ref[...].astype(BF16), w_ref[0].astype(BF16))


def _memkv(mem, w_mem_kv):
    b, m, d = mem.shape
    depth, _, n = w_mem_kv.shape
    return pl.pallas_call(
        _memkv_body,
        grid=(depth,),
        in_specs=[pl.BlockSpec((b * m, d), lambda l: (0, 0)),
                  pl.BlockSpec((1, d, n), lambda l: (l, 0, 0))],
        out_specs=pl.BlockSpec((1, b * m, n), lambda l: (l, 0, 0)),
        out_shape=jax.ShapeDtypeStruct((depth, b * m, n), F32),
        compiler_params=pltpu.CompilerParams(vmem_limit_bytes=VMEM_LIMIT),
        name="mem_kv_proj",
    )(mem.reshape(b * m, d), w_mem_kv)


def _proj_body(x_ref, wt_ref, wr_ref, qn_ref, kvn_ref, wqbt_ref, wvt_ref, wkmla_ref,
               cos_ref, sin_ref, fbias_ref, selq_ref, onesq_ref, selk_ref, onesk_ref,
               kmem_ref, vmemt_ref,
               qmla_o, kmla_o, vmla_o, qsb_o, ksb_o, vsb_o, qfox_o, kfox_o, vfox_o, ymem_o, gate_o, mr_o,
               kn_o, cend_o, carry_ref):
    tm = x_ref.shape[1]
    xb = x_ref[0].astype(BF16)

    def store_vt(o_ref, vt):
        tk = o_ref.shape[3]
        for j in range(tm // tk):
            o_ref[0, j] = vt[:, j * tk:(j + 1) * tk].astype(BF16)

    def rms(v, g):
        return v * lax.rsqrt(jnp.mean(v * v, axis=0, keepdims=True) + RMS_EPS) * g

    ff = _dot_nt(wt_ref[_T_FF:_T_END, :], xb) + fbias_ref[...]
    lf = jnp.minimum(ff, 0.0) - jnp.log1p(jnp.exp(-jnp.abs(ff)))
    upper = (lax.broadcasted_iota(jnp.int32, (tm, tm), 0) <= lax.broadcasted_iota(jnp.int32, (tm, tm), 1))
    cqn = rms(_dot_nt(wt_ref[_T_CQ:_T_CKV, :], xb), qn_ref[...]).astype(BF16)
    ckvn = rms(_dot_nt(wt_ref[_T_CKV:_T_SQ, :], xb), kvn_ref[...])
    ckvn_t, ckvn_r = ckvn.astype(BF16), ckvn.T.astype(BF16)
    cos_t, sin_t = cos_ref[0], sin_ref[0]
    tab = jnp.concatenate([cos_t, cos_t, sin_t, sin_t, jnp.zeros((HEAD_PAD - 2 * MLA_ROPE, tm), F32)], axis=0).T
    prod = _dot(xb, wr_ref[:, _R_ROPE:_R_SK]) * tab
    lane = lax.broadcasted_iota(jnp.int32, prod.shape, 1)
    kpe = jnp.where(lane < MLA_ROPE, prod + pltpu.roll(prod, HEAD_PAD - MLA_ROPE, 1), 0.0)
    kpe = pltpu.roll(kpe, MLA_NOPE, 1)
    mq = (_dot_nt(wt_ref[_T_MQ:_T_FF, :], xb) * (HEAD_V ** -0.5)).astype(BF16)

    csum = _dot(jnp.concatenate(_split3(lf), axis=0).astype(BF16), upper.astype(BF16))

    @pl.when(pl.program_id(1) == 0)
    def _():
        carry_ref[...] = jnp.zeros_like(carry_ref)

    c = csum[0:AUG] + csum[AUG:2 * AUG] + csum[2 * AUG:3 * AUG] + carry_ref[:, 0:1]
    carry_ref[...] = jnp.broadcast_to(c[:, tm - 1:tm], carry_ref.shape)
    c_log2 = c * LOG2E
    cst = jnp.concatenate([*_split3(c_log2), jnp.zeros((HEAD_PAD - 3 * AUG, tm), F32)], axis=0)
    tk_fox = vfox_o.shape[3]
    for j in range(tm // tk_fox):
        cend_o[0, j] = jnp.broadcast_to(c_log2[:, (j + 1) * tk_fox - 1:(j + 1) * tk_fox], cend_o.shape[2:])

    sq = _dot_nt(wt_ref[_T_SQ:_T_SV, :], xb) * (SB_HEAD_DIM ** -0.5 * LOG2E)
    zs = jnp.zeros((HEAD_PAD - SB_HEAD_DIM, tm), F32)
    for h in range(HEADS):
        qh = sq[h * SB_HEAD_DIM:(h + 1) * SB_HEAD_DIM]
        blk = jnp.concatenate([qh, zs] if h % 2 == 0 else [zs, qh], axis=0)
        qsb_o[0, h * HEAD_PAD:(h + 1) * HEAD_PAD, :] = blk.astype(BF16)
    q = _dot(wqbt_ref[...], cqn) * ((MLA_NOPE + MLA_ROPE) ** -0.5 * LOG2E)
    half = MLA_ROPE // 2
    dq = MLA_NOPE + MLA_ROPE
    zq = jnp.zeros((HEAD_PAD - dq, tm), F32)
    for h in range(HEADS):
        nope = q[h * dq:h * dq + MLA_NOPE]
        x1 = q[h * dq + MLA_NOPE:h * dq + MLA_NOPE + half]
        x2 = q[h * dq + MLA_NOPE + half:(h + 1) * dq]
        blk = jnp.concatenate([nope, x1 * cos_t - x2 * sin_t, x1 * sin_t + x2 * cos_t, zq], axis=0)
        qmla_o[0, h * HEAD_PAD:(h + 1) * HEAD_PAD, :] = blk.astype(BF16)

    store_vt(vsb_o, _dot_nt(wt_ref[_T_SV:_T_FQ, :], xb))
    store_vt(vmla_o, _dot(wvt_ref[...], ckvn_t))
    ksb_o[0] = _dot(xb, wr_ref[:, _R_SK:_R_FK]).astype(BF16)
    kmla_o[0] = (_dot(ckvn_r, wkmla_ref[...]) + jnp.concatenate([kpe] * HEADS, axis=1)).astype(BF16)

    fq = _dot_nt(wt_ref[_T_FQ:_T_FV, :], xb) * (SB_HEAD_DIM ** -0.5 * LOG2E)
    aug_t = _dot(selq_ref[...], cst.astype(BF16)) + onesq_ref[...]
    zf = jnp.zeros((HEAD_PAD - SB_HEAD_DIM - 8, tm), F32)
    for h in range(HEADS):
        blk = jnp.concatenate([fq[h * SB_HEAD_DIM:(h + 1) * SB_HEAD_DIM], aug_t[h * 8:(h + 1) * 8], zf], axis=0)
        qfox_o[0, h * HEAD_PAD:(h + 1) * HEAD_PAD, :] = blk.astype(BF16)
    store_vt(vfox_o, _dot_nt(wt_ref[_T_FV:_T_MQ, :], xb))
    kfox = _dot(xb, wr_ref[:, _R_FK:_R_GATE]) + _dot(cst.T.astype(BF16), selk_ref[...]) + onesk_ref[...]
    kfox = kfox.astype(BF16)
    kfox_o[0] = kfox
    ksq = jnp.square(kfox.astype(F32))
    group = lax.broadcasted_iota(jnp.int32, kn_o.shape[2:], 1) // 32
    kn = jnp.zeros(kn_o.shape[2:], F32)
    for h in range(HEADS):
        n2 = jnp.sum(ksq[:, h * HEAD_PAD:h * HEAD_PAD + SB_HEAD_DIM], axis=1, keepdims=True)
        kn = jnp.where(group == h, jnp.max(n2, axis=0, keepdims=True), kn)
    kn_o[0, 0] = kn

    s_mem = [_dot(kmem_ref[0, h], mq[h * HEAD_V:(h + 1) * HEAD_V]) for h in range(HEADS)]
    gate_o[0] = _dot(xb, wr_ref[:, _R_GATE:_R_MR]).astype(BF16)
    mr_o[0] = _dot(xb, wr_ref[:, _R_MR:_R_END]).astype(BF16)
    for h in range(HEADS):
        p = jnp.exp(s_mem[h] - jnp.max(s_mem[h], axis=0, keepdims=True))
        o = _dot(vmemt_ref[0, h], p.astype(BF16))
        ymem_o[0, h * HEAD_V:(h + 1) * HEAD_V, :] = o / jnp.sum(p, axis=0, keepdims=True)


def _placed(w, n_in):
    depth, d, _ = w.shape
    w = w.reshape(depth, d, HEADS, n_in)
    return jnp.pad(w, ((0, 0), (0, 0), (0, 0), (0, HEAD_PAD - n_in))).reshape(depth, d, HEADS * HEAD_PAD)


def _selection_constants():
    selq = np.zeros((HEADS * 8, HEAD_PAD), np.float32)
    onesq = np.zeros((HEADS * 8, 1), np.float32)
    selk = np.zeros((HEAD_PAD, HEADS * HEAD_PAD), np.float32)
    onesk = np.zeros((1, HEADS * HEAD_PAD), np.float32)
    for h in range(HEADS):
        for i in range(3):
            selq[h * 8 + i, i * AUG + h] = 1.0
            onesq[h * 8 + 3 + i, 0] = 1.0
            onesk[0, h * HEAD_PAD + SB_HEAD_DIM + i] = 1.0
            selk[i * AUG + h, h * HEAD_PAD + SB_HEAD_DIM + 3 + i] = -1.0
    return jnp.asarray(selq, BF16), jnp.asarray(onesq), jnp.asarray(selk, BF16), jnp.asarray(onesk)


def _proj_weights(w_in, q_norm, w_qb, kv_norm, w_kvb, fox_bias):
    depth = w_in.shape[0]
    cols = lambda a, n: w_in[:, :, a:a + n]
    t = lambda a: jnp.swapaxes(a, 1, 2)
    wt = jnp.concatenate([
        t(cols(_C_CQ, MLA_Q_RANK)), t(cols(_C_CKV, MLA_KV_RANK)), t(cols(_C_SQ, BRANCH_WIDTH)),
        t(cols(_C_SV, BRANCH_WIDTH)), t(cols(_C_FQ, BRANCH_WIDTH)), t(cols(_C_FV, BRANCH_WIDTH)),
        t(cols(_C_MQ, BRANCH_WIDTH)), t(cols(_C_FF, HEADS)),
        jnp.zeros((depth, _T_END - _T_FF - HEADS, D_MODEL), F32)], axis=1).astype(BF16)
    kr = cols(_C_KR, MLA_ROPE)
    half = MLA_ROPE // 2
    rope = jnp.concatenate([kr, -kr[:, :, half:], kr[:, :, :half],
                            jnp.zeros((depth, D_MODEL, HEAD_PAD - 2 * MLA_ROPE), F32)], axis=2)
    wr = jnp.concatenate([
        rope, cols(_C_SK, BRANCH_WIDTH), _placed(cols(_C_FK, BRANCH_WIDTH), SB_HEAD_DIM),
        cols(_C_GATE, N_BRANCHES * BRANCH_WIDTH), cols(_C_MR, MERGE_RANK)], axis=2).astype(BF16)
    kvb = w_kvb.reshape(depth, MLA_KV_RANK, HEADS, MLA_NOPE + HEAD_V)
    wkmla = jnp.pad(kvb[..., :MLA_NOPE], ((0, 0), (0, 0), (0, 0), (0, HEAD_PAD - MLA_NOPE)))
    wkmla = wkmla.reshape(depth, MLA_KV_RANK, HEADS * HEAD_PAD)
    wvt = t(kvb[..., MLA_NOPE:].reshape(depth, MLA_KV_RANK, HEADS * HEAD_V))
    fbias = jnp.pad(fox_bias, ((0, 0), (0, AUG - HEADS)))[:, :, None]
    return (wt, wr, q_norm[:, :, None], kv_norm[:, :, None],
            t(w_qb).astype(BF16), wvt.astype(BF16), wkmla.astype(BF16), fbias)


def _layer_spec(a, l):
    return pl.BlockSpec((None,) + a.shape[1:], lambda i, j: (l,) + (0,) * (a.ndim - 1))


def _proj(x, l, weights, consts, cos_t, sin_t, kmem, vmemt):
    b, s, d = x.shape
    lay = lambda a: _layer_spec(a, l)
    wt, wr, qn, kvn, wqbt, wvt, wkmla, fbias = weights
    selq, onesq, selk, onesk = consts
    full = lambda a: pl.BlockSpec(a.shape, lambda i, j: (0,) * a.ndim)
    qt_spec = pl.BlockSpec((1, HEADS * HEAD_PAD, TM), lambda i, j: (i, 0, j))
    k_spec = lambda w: pl.BlockSpec((1, TM, w), lambda i, j: (i, j, 0))
    vt_spec = lambda tk: pl.BlockSpec((1, TM // tk, HEADS * HEAD_V, tk), lambda i, j: (i, j, 0, 0))
    qt_shape = jax.ShapeDtypeStruct((b, HEADS * HEAD_PAD, s), BF16)
    k_shape = lambda w: jax.ShapeDtypeStruct((b, s, w), BF16)
    kw, kw_sb = HEADS * HEAD_PAD, HEADS * SB_HEAD_DIM
    vt_shape = lambda tk: jax.ShapeDtypeStruct((b, s // tk, HEADS * HEAD_V, tk), BF16)
    return pl.pallas_call(
        _proj_body,
        grid=(b, s // TM),
        in_specs=[pl.BlockSpec((1, TM, d), lambda i, j: (i, j, 0)),
                  lay(wt), lay(wr), lay(qn), lay(kvn), lay(wqbt), lay(wvt), lay(wkmla),
                  pl.BlockSpec((1, MLA_ROPE // 2, TM), lambda i, j: (i, 0, j)),
                  pl.BlockSpec((1, MLA_ROPE // 2, TM), lambda i, j: (i, 0, j)),
                  lay(fbias), full(selq), full(onesq), full(selk), full(onesk),
                  pl.BlockSpec((None, 1, HEADS, MEM_LEN, HEAD_V), lambda i, j: (l, i, 0, 0, 0)),
                  pl.BlockSpec((None, 1, HEADS, HEAD_V, MEM_LEN), lambda i, j: (l, i, 0, 0, 0))],
        out_specs=[qt_spec, k_spec(kw), vt_spec(TK_SM), qt_spec, k_spec(kw_sb), vt_spec(TK_SB), qt_spec, k_spec(kw),
                   vt_spec(TK_SM),
                   pl.BlockSpec((1, HEADS * HEAD_V, TM), lambda i, j: (i, 0, j)),
                   pl.BlockSpec((1, TM, N_BRANCHES * BRANCH_WIDTH), lambda i, j: (i, j, 0)),
                   pl.BlockSpec((1, TM, MERGE_RANK), lambda i, j: (i, j, 0)),
                   pl.BlockSpec((1, 1, SUBLANES, LANES), lambda i, j: (i, j, 0, 0)),
                   pl.BlockSpec((1, TM // TK_SM, AUG, LANES), lambda i, j: (i, j, 0, 0))],
        out_shape=[qt_shape, k_shape(kw), vt_shape(TK_SM), qt_shape, k_shape(kw_sb), vt_shape(TK_SB), qt_shape,
                   k_shape(kw), vt_shape(TK_SM),
                   jax.ShapeDtypeStruct((b, HEADS * HEAD_V, s), F32),
                   jax.ShapeDtypeStruct((b, s, N_BRANCHES * BRANCH_WIDTH), BF16),
                   jax.ShapeDtypeStruct((b, s, MERGE_RANK), BF16),
                   jax.ShapeDtypeStruct((b, s // TM, SUBLANES, LANES), F32),
                   jax.ShapeDtypeStruct((b, s // TK_SM, AUG, LANES), F32)],
        scratch_shapes=[pltpu.VMEM((AUG, LANES), F32)],
        compiler_params=pltpu.CompilerParams(dimension_semantics=("arbitrary", "arbitrary"),
                                             vmem_limit_bytes=VMEM_LIMIT),
        name="fused_projection",
    )(x, wt, wr, qn, kvn, wqbt, wvt, wkmla, cos_t, sin_t, fbias, selq, onesq, selk, onesk, kmem, vmemt)


def _causal_masks(tq, tk, strict):
    row = lax.broadcasted_iota(jnp.int32, (tk, tq), 0)
    col = lax.broadcasted_iota(jnp.int32, (tk, tq), 1)
    return [(row + j * tk < col) if strict else (row + j * tk <= col) for j in range(tq // tk)]


def _softmax_attn_body(q_ref, k_ref, v_ref, *rest, tq, tk, hb, decay):
    if decay:
        kn_ref, cend_ref, o_ref, acc_ref, s_ref = rest
    else:
        o_ref, acc_ref, s_ref = rest
    nd = tq // tk
    qi = pl.program_id(2)
    n_full = qi * nd
    acc_ref[...] = jnp.zeros_like(acc_ref)

    def scores(k, h, lo):
        rows = pl.ds(pl.multiple_of(k * tk, tk), tk)
        return _dot(k_ref[0, rows, h * HEAD_PAD:(h + 1) * HEAD_PAD], q_ref[0, h * HEAD_PAD:(h + 1) * HEAD_PAD, lo:])

    ones = jnp.ones((ACC_ROWS - HEAD_V, tk), BF16)

    def step(k, slot, carry, mask=None, lo=0, lo_next=0):
        k_next = jnp.maximum(k - 1, 0)
        out = [None] * hb

        def prefetch(h):
            if h < hb:
                s_ref[1 - slot, h, :, lo_next:] = scores(k_next, h, lo_next)

        def softmax_pv(h):
            m = carry[h][:, lo:]
            s = s_ref[slot, h, :, lo:]
            if mask is not None:
                s = jnp.where(mask[:, lo:], s, NEG)
            m_new = jnp.maximum(m, jnp.max(s, axis=0, keepdims=True))
            p = jnp.exp2(s - m_new)
            if mask is not None:
                p = jnp.where(mask[:, lo:], p, 0.0)
            v1 = jnp.concatenate([v_ref[0, k, h * HEAD_V:(h + 1) * HEAD_V, :], ones], axis=0)
            ha = slice(h * ACC_ROWS, (h + 1) * ACC_ROWS)
            acc_ref[ha, lo:] = jnp.exp2(m - m_new) * acc_ref[ha, lo:] + _dot(v1, p.astype(BF16))
            out[h] = m_new if lo == 0 else jnp.concatenate([carry[h][:, :lo], m_new], axis=1)

        prefetch(0)
        for h in range(hb):
            prefetch(h + 1)
            softmax_pv(h)
        return tuple(out)

    for h in range(hb):
        s_ref[(nd - 1) % 2, h, :, (nd - 1) * tk:] = scores(n_full + nd - 1, h, (nd - 1) * tk)
    carry = tuple(jnp.full((1, tq), NEG, F32) for _ in range(hb))
    masks = _causal_masks(tq, tk, strict=False)
    for j in reversed(range(nd)):
        carry = step(n_full + j, j % 2, carry, masks[j], lo=j * tk, lo_next=max(j - 1, 0) * tk)

    def pair(i, carry):
        k = n_full - 1 - 2 * i
        return step(k - 1, 0, step(k, 1, carry))

    if not decay:
        n_pairs = n_full // 2
        quad = lambda i, c: pair(4 * i + 3, pair(4 * i + 2, pair(4 * i + 1, pair(4 * i, c))))
        carry = lax.fori_loop(0, n_pairs // 4, quad, carry)
        carry = lax.fori_loop(n_pairs - n_pairs % 4, n_pairs, pair, carry)
    else:
        assert hb == HEADS
        kmax = jnp.sqrt(jnp.max(kn_ref[0], axis=0)[0:1, :]) * 1.01
        reach = []
        for h in range(hb):
            qh = q_ref[0, h * HEAD_PAD:h * HEAD_PAD + SB_HEAD_DIM + 8, :].astype(F32)
            qnorm = jnp.sqrt(jnp.sum(jnp.square(qh[0:SB_HEAD_DIM]), axis=0, keepdims=True))
            c_t = jnp.sum(qh[SB_HEAD_DIM:SB_HEAD_DIM + 3], axis=0, keepdims=True)
            reach.append(qnorm * kmax[:, 32 * h:32 * h + 1] + c_t)

        def alive(carry, k_low):
            c_end = cend_ref[0, jnp.maximum(k_low - 1, 0)]
            top = None
            for h in range(hb):
                gap = reach[h] - c_end[h:h + 1, 0:1] - carry[h]
                top = gap if top is None else jnp.maximum(top, gap)
            return jnp.max(top) > DEAD_LOG2

        def cond(state):
            i, live, _ = state
            return jnp.logical_and(i < n_full // 2, live)

        def body(state):
            i, _, carry = state
            carry = pair(i, carry)
            return i + 1, alive(carry, n_full - 2 * (i + 1)), carry

        carry = lax.while_loop(cond, body, (jnp.int32(0), alive(carry, n_full), carry))[2]
    for h in range(hb):
        num = acc_ref[h * ACC_ROWS:h * ACC_ROWS + HEAD_V, :]
        den = acc_ref[h * ACC_ROWS + HEAD_V:h * ACC_ROWS + HEAD_V + 1, :]
        o_ref[0, h * HEAD_V:(h + 1) * HEAD_V, :] = num * (1.0 / den)


def _sb_attn_body(q_ref, k_ref, v_ref, ntri_ref, o_ref, acc_ref, s_ref, *, tq, tk, hb):
    nd = tq // tk
    qi = pl.program_id(2)
    n_full = qi * nd
    acc_ref[...] = jnp.zeros_like(acc_ref)

    def scores(k, h, lo):
        rows = pl.ds(pl.multiple_of(k * tk, tk), tk)
        pair_lanes = slice((h // 2) * HEAD_PAD, (h // 2 + 1) * HEAD_PAD)
        return _dot(k_ref[0, rows, pair_lanes], q_ref[0, h * HEAD_PAD:(h + 1) * HEAD_PAD, lo:])

    def step(k, slot, carry, mask=None, lo=0, lo_next=0, gate=None):
        k_next = jnp.maximum(k - 1, 0)
        k = jnp.maximum(k, 0)
        out = [None] * hb
        local = [None] * hb

        def suffix(h):
            if h >= hb:
                return
            z = s_ref[slot, h, :, lo:]
            neg_abs = pltpu.bitcast(pltpu.bitcast(z, jnp.uint32) | jnp.uint32(0x80000000), F32)
            sp = jnp.maximum(z, 0.0) + jnp.log(1.0 + jnp.exp2(neg_abs)) * LOG2E
            if mask is not None:
                sp = jnp.where(mask[:, lo:], sp, 0.0)
            local[h] = _dot(ntri_ref[...], sp.astype(BF16))
            s_ref[1 - slot, h, :, lo_next:] = scores(k_next, h, lo_next)

        def weights_pv(h):
            cross = carry[h][:, lo:]
            a = jnp.exp2(s_ref[slot, h, :, lo:] + local[h])
            if mask is not None:
                a = jnp.where(mask[:, lo:], a, 0.0)
            hv = slice(h * HEAD_V, (h + 1) * HEAD_V)
            scale, total = jnp.exp2(cross), local[h][0:1, :]
            if gate is not None:
                scale, total = scale * gate, total * gate
            acc_ref[hv, lo:] += _dot(v_ref[0, k, hv, :], a.astype(BF16)) * scale
            cross = cross + total
            out[h] = cross if lo == 0 else jnp.concatenate([carry[h][:, :lo], cross], axis=1)

        suffix(0)
        for h in range(hb):
            suffix(h + 1)
            weights_pv(h)
        return tuple(out)

    for h in range(hb):
        s_ref[(nd - 1) % 2, h, :, (nd - 1) * tk:] = scores(n_full + nd - 1, h, (nd - 1) * tk)
    carry = tuple(jnp.zeros((1, tq), F32) for _ in range(hb))
    masks = _causal_masks(tq, tk, strict=True)
    for j in reversed(range(nd)):
        carry = step(n_full + j, j % 2, carry, masks[j], lo=j * tk, lo_next=max(j - 1, 0) * tk)

    def pair(i, carry, gate=None):
        k = n_full - 1 - 2 * i
        return step(k - 1, 0, step(k, 1, carry, gate=gate), gate=gate)

    def alive(carry):
        top = functools.reduce(jnp.maximum, carry)
        return jnp.max(top) > DEAD_LOG2

    def cond(state):
        i, live, _ = state
        return jnp.logical_and(i < n_full // 2, live)

    def body(state):
        i, _, carry = state
        carry = pair(i, carry)
        return i + 1, alive(carry), carry

    carry = pair(0, carry, gate=(qi > 0).astype(F32))
    lax.while_loop(cond, body, (jnp.int32(1), alive(carry), carry))
    o_ref[0] = acc_ref[...]


def _attention(name, body, q_t, k, v_t, tq, tk, hb, acc_rows, extra=()):
    b, _, s = q_t.shape
    in_specs = [pl.BlockSpec((1, hb * HEAD_PAD, tq), lambda i, g, j: (i, g, j)),
                pl.BlockSpec((1, s, k.shape[2]), lambda i, g, j: (i, 0, g)),
                pl.BlockSpec((1, s // tk, hb * HEAD_V, tk), lambda i, g, j: (i, 0, g, 0))]
    in_specs += [spec for _, spec in extra]
    return pl.pallas_call(
        functools.partial(body, tq=tq, tk=tk, hb=hb),
        grid=(b, HEADS // hb, s // tq),
        in_specs=in_specs,
        out_specs=pl.BlockSpec((1, hb * HEAD_V, tq), lambda i, g, j: (i, g, j)),
        out_shape=jax.ShapeDtypeStruct((b, HEADS * HEAD_V, s), F32),
        scratch_shapes=[pltpu.VMEM((hb * acc_rows, tq), F32), pltpu.VMEM((2, hb, tk, tq), F32)],
        compiler_params=pltpu.CompilerParams(dimension_semantics=("arbitrary",) * 3,
                                             vmem_limit_bytes=VMEM_LIMIT),
        name=name,
    )(q_t, k, v_t, *[a for a, _ in extra])


def _out_body(x_ref, ymla_ref, ysb_ref, yfox_ref, ymem_ref, gate_ref, mr_ref, wmu_ref, wb_ref, wo_ref,
              g_ref, b_ref, o_ref):
    tm = x_ref.shape[1]
    parts = [slice(i * (tm // OUT_SPLIT), (i + 1) * (tm // OUT_SPLIT)) for i in range(OUT_SPLIT)]

    def merge(rows):
        mr = mr_ref[0, rows, :]
        merged = None
        for n, y_ref in enumerate((ymla_ref, ysb_ref, yfox_ref, ymem_ref)):
            y = y_ref[0, :, rows].T
            gz = gate_ref[0, rows, n * BRANCH_WIDTH:(n + 1) * BRANCH_WIDTH].astype(F32)
            yb = (y * (gz * _sigmoid(gz))).astype(BF16)
            branch = _dot(yb, wb_ref[n])
            mg = _sigmoid(_dot(mr, wmu_ref[:, n * D_MODEL:(n + 1) * D_MODEL]))
            merged = mg * branch if merged is None else merged + mg * branch
        return merged.astype(BF16)

    merged = [merge(rows) for rows in parts]
    outs = [_dot(m, wo_ref[...]) for m in merged]
    for rows, out in zip(parts, outs):
        r = DEEPNORM_ALPHA * x_ref[0, rows, :] + out
        mu = jnp.mean(r, axis=-1, keepdims=True)
        var = jnp.mean(jnp.square(r - mu), axis=-1, keepdims=True)
        o_ref[0, rows, :] = (r - mu) * lax.rsqrt(var + LN_EPS) * g_ref[...] + b_ref[...]


def _out(x, l, ys, gate, mr, wmu, wb, wo, g, bias):
    b, s, d = x.shape
    lay = lambda a: _layer_spec(a, l)
    yt_spec = pl.BlockSpec((1, BRANCH_WIDTH, TM), lambda i, j: (i, 0, j))
    return pl.pallas_call(
        _out_body,
        grid=(b, s // TM),
        in_specs=[pl.BlockSpec((1, TM, d), lambda i, j: (i, j, 0)), yt_spec, yt_spec, yt_spec, yt_spec,
                  pl.BlockSpec((1, TM, N_BRANCHES * BRANCH_WIDTH), lambda i, j: (i, j, 0)),
                  pl.BlockSpec((1, TM, MERGE_RANK), lambda i, j: (i, j, 0)),
                  lay(wmu), lay(wb), lay(wo), lay(g), lay(bias)],
        out_specs=pl.BlockSpec((1, TM, d), lambda i, j: (i, j, 0)),
        out_shape=jax.ShapeDtypeStruct((b, s, d), F32),
        compiler_params=pltpu.CompilerParams(dimension_semantics=("arbitrary", "arbitrary"),
                                             vmem_limit_bytes=VMEM_LIMIT),
        name="fused_output",
    )(x, *ys, gate, mr, wmu, wb, wo, g, bias)


def kernel(x, mem, positions, w_in, mla_q_norm, mla_w_qb, mla_kv_norm, mla_w_kvb, fox_forget_bias, w_mem_kv,
           w_merge_up, w_branch, w_out, ln_gain, ln_bias):
    b, s, d = x.shape
    assert d == D_MODEL and s % TM == 0 and s % TQ == 0
    assert all(tq % (2 * tk) == 0 and TM % tk == 0 and s % tq == 0 for tq, tk in ((TQ, TK_SM), (TQ_SB, TK_SB)))
    assert HB == HEADS
    depth = w_in.shape[0]

    cos_t, sin_t = _rope_tables(positions)

    mkv = _memkv(mem, w_mem_kv).reshape(depth, b, MEM_LEN, 2, HEADS, HEAD_V)
    kmem = mkv[:, :, :, 0].transpose(0, 1, 3, 2, 4).astype(BF16)
    vmemt = mkv[:, :, :, 1].transpose(0, 1, 3, 4, 2).astype(BF16)

    consts = _selection_constants()
    ntri = jnp.asarray(-np.triu(np.ones((TK_SB, TK_SB), np.float32)), BF16)
    weights = _proj_weights(w_in, mla_q_norm, mla_w_qb, mla_kv_norm, mla_w_kvb, fox_forget_bias)
    wmu, wb, wo = w_merge_up.astype(BF16), w_branch.astype(BF16), w_out.astype(BF16)
    ln_g, ln_b = ln_gain[:, None, :], ln_bias[:, None, :]
    whole = lambda a: pl.BlockSpec((1,) + a.shape[1:], lambda i, g, j: (i,) + (0,) * (a.ndim - 1))

    for l in range(depth):
        (qmla, kmla, vmla, qsb, ksb, vsb, qfox, kfox, vfox, ymem, gate, mr, kn, cend) = _proj(
            x, l, weights, consts, cos_t, sin_t, kmem, vmemt)
        ymla = _attention("mla_attention", functools.partial(_softmax_attn_body, decay=False), qmla, kmla, vmla,
                          TQ, TK_SM, HB, ACC_ROWS)
        ysb = _attention("stick_breaking_attention", _sb_attn_body, qsb, ksb, vsb, TQ_SB, TK_SB, HB, HEAD_V,
                         extra=((ntri, pl.BlockSpec(ntri.shape, lambda i, g, j: (0, 0))),))
        yfox = _attention("forgetting_attention", functools.partial(_softmax_attn_body, decay=True), qfox, kfox, vfox,
                          TQ, TK_SM, HB, ACC_ROWS, extra=((kn, whole(kn)), (cend, whole(cend))))
        x = _out(x, l, (ymla, ysb, yfox, ymem), gate, mr, wmu, wb, wo, ln_g, ln_b)
    return x
```
